```python
import functools
import jax, jax.numpy as jnp
from jax import lax
import numpy as np

D_MODEL = 2048
BATCH = 2
SEQ = 4096
DEPTH = 1
DEC_BATCH = 8
DEC_SEQ = 4
PAST_LEN = 16384
PAGE_SIZE = 128

HEAD_DIM = 64
D_CONV = D_MODEL // 2
N_HEADS = (D_MODEL // 2) // HEAD_DIM
D_ATTN = N_HEADS * HEAD_DIM
D_MIX = D_CONV + D_ATTN
CONV_A_WIDTH = 31
DIL_PATTERNS = ((128, 1), (512, 4), (2048, 16))
WIN_MAX = max(w for w, _ in DIL_PATTERNS)
ROPE_THETA = 10000.0
D_FF = 5632
FFN_CONV_WIDTH = 3
EPS = 1e-6
NEG_INF = -1e30

kernel_name = 'hymba_conformer_dilated_convffn_step'


def rms_norm(x, g):
    xf = x.astype(jnp.float32)
    y = xf * lax.rsqrt(jnp.mean(xf * xf, axis=-1, keepdims=True) + EPS) * g.astype(jnp.float32)
    return y.astype(x.dtype)


def layer_norm(x, g, b):
    xf = x.astype(jnp.float32)
    xc = xf - jnp.mean(xf, axis=-1, keepdims=True)
    y = xc * lax.rsqrt(jnp.mean(xc * xc, axis=-1, keepdims=True) + EPS) * g.astype(jnp.float32) + b.astype(jnp.float32)
    return y.astype(x.dtype)


def rope(x, pos):
    half = HEAD_DIM // 2
    inv = ROPE_THETA ** (-jnp.arange(half, dtype=jnp.float32) / half)
    ang = pos.astype(jnp.float32)[:, None] * inv[None, :]
    cos = jnp.cos(ang)[None, :, None, :]
    sin = jnp.sin(ang)[None, :, None, :]
    xf = x.astype(jnp.float32)
    x1, x2 = xf[..., :half], xf[..., half:]
    return jnp.concatenate([x1 * cos - x2 * sin, x2 * cos + x1 * sin], axis=-1).astype(x.dtype)


def causal_dwconv(prev, x, w, b):
    xx = jnp.concatenate([prev.astype(x.dtype), x], axis=1)
    y = lax.conv_general_dilated(xx, w[:, None, :].astype(x.dtype), window_strides=(1,), padding='VALID',
                                 dimension_numbers=('NWC', 'WIO', 'NWC'), feature_group_count=x.shape[-1])
    return y + b.astype(x.dtype), xx[:, -(w.shape[0] - 1):]


def dilated_band_attention(q, k, v, window, dil):
    B, S, H, Dh = q.shape
    blk = window // dil
    span = blk * dil
    Sp = -(-S // span) * span
    nb = Sp // span

    def to_blocks(a):
        a = jnp.pad(a.astype(jnp.float32), ((0, 0), (0, Sp - S), (0, 0), (0, 0)))
        return a.reshape(B, nb, blk, dil, H, Dh)

    def with_prev(a):
        prev = jnp.pad(a[:, :-1], ((0, 0), (1, 0), (0, 0), (0, 0), (0, 0), (0, 0)))
        return jnp.concatenate([prev, a], axis=2)

    qb = to_blocks(q)
    kc = with_prev(to_blocks(k))
    vc = with_prev(to_blocks(v))
    s = jnp.einsum('bnqrhd,bnkrhd->bnrhqk', qb, kc) * (Dh ** -0.5)
    qi = jnp.arange(blk)[:, None]
    ki = jnp.arange(2 * blk)[None, :]
    band = (ki >= qi) & (ki <= qi + blk)
    has_prev = (jnp.arange(nb) > 0)[:, None, None] | (ki >= blk)[None]
    mask = band[None] & has_prev
    s = jnp.where(mask[None, :, None, None], s, NEG_INF)
    lse = jax.nn.logsumexp(s, axis=-1)
    p = jnp.exp(s - lse[..., None])
    o = jnp.einsum('bnrhqk,bnkrhd->bnqrhd', p, vc).reshape(B, Sp, H, Dh)[:, :S]
    lse = lse.transpose(0, 1, 4, 2, 3).reshape(B, Sp, H)[:, :S]
    return o, lse


def merge_by_denominator(outs, lses):
    w = jax.nn.softmax(jnp.stack(lses), axis=0)
    return jnp.einsum('pblh,pblhd->blhd', w, jnp.stack(outs))


def prompt_attend(q, k, v):
    outs, lses = [], []
    for window, dil in DIL_PATTERNS:
        o, lse = dilated_band_attention(q, k, v, window, dil)
        outs.append(o)
        lses.append(lse)
    return merge_by_denominator(outs, lses).astype(v.dtype)


def sample_attend(q, k, v, k_buf, v_buf):
    kc = jnp.concatenate([k_buf, k.astype(k_buf.dtype)], axis=1).astype(jnp.float32)
    vc = jnp.concatenate([v_buf, v.astype(v_buf.dtype)], axis=1).astype(jnp.float32)
    wb = k_buf.shape[1]
    T = q.shape[1]
    qf = q.astype(jnp.float32)
    outs, lses = [], []
    for window, dil in DIL_PATTERNS:
        nk = window // dil + 1
        idx = wb + jnp.arange(T)[:, None] - dil * jnp.arange(nk)[None, :]
        valid = idx >= 0
        idx = jnp.maximum(idx, 0)
        kg = jnp.take(kc, idx, axis=1)
        vg = jnp.take(vc, idx, axis=1)
        s = jnp.einsum('bthd,btjhd->bthj', qf, kg) * (HEAD_DIM ** -0.5)
        s = jnp.where(valid[None, :, None, :], s, NEG_INF)
        lse = jax.nn.logsumexp(s, axis=-1)
        p = jnp.exp(s - lse[..., None])
        outs.append(jnp.einsum('bthj,btjhd->bthd', p, vg))
        lses.append(lse)
    return merge_by_denominator(outs, lses).astype(v.dtype)


def trunk_layer(x, c, pos, a_prev, f_prev, attend, p):
    mod = (jax.nn.silu(c) @ p['w_ada'] + p['b_ada'])[:, None, :]
    sh1, sc1, g1, sh2, sc2, g2 = jnp.split(mod, 6, axis=-1)
    B, L = x.shape[0], x.shape[1]

    h = rms_norm(x, p['norm_mix_g']) * (1 + sc1) + sh1
    proj = h @ p['w_in']
    a_val, a_gate, q, k, v = jnp.split(
        proj, [D_CONV, 2 * D_CONV, 2 * D_CONV + D_ATTN, 2 * D_CONV + 2 * D_ATTN], axis=-1)

    g = a_val * jax.nn.sigmoid(a_gate)
    gc, a_tail = causal_dwconv(a_prev, g, p['conv_a_w'], p['conv_a_b'])
    a_out = jax.nn.silu(layer_norm(gc, p['ln_a_g'], p['ln_a_b']))

    q = rope(rms_norm(q.reshape(B, L, N_HEADS, HEAD_DIM), p['q_norm_g']), pos)
    k = rope(rms_norm(k.reshape(B, L, N_HEADS, HEAD_DIM), p['k_norm_g']), pos)
    v = v.reshape(B, L, N_HEADS, HEAD_DIM)
    o = attend(q, k, v).reshape(B, L, D_ATTN)

    x = x + g1 * (jnp.concatenate([a_out, o], axis=-1) @ p['w_out'])

    h2 = rms_norm(x, p['norm_ffn_g']) * (1 + sc2) + sh2
    u, f_tail = causal_dwconv(f_prev, h2 @ p['w_up'], p['ffn_conv_w'], p['ffn_conv_b'])
    gate, val = jnp.split(u, 2, axis=-1)
    x = x + g2 * ((jax.nn.silu(gate) * val) @ p['w_down'])
    return x, k, v, a_tail, f_tail


def setup_inputs(seed: int = 0) -> dict:
    key = jax.random.key(seed)
    ks = jax.random.split(key, 26)

    def nrm(k, shape, s):
        return jax.random.normal(k, shape, jnp.float32) * s

    wb = min(WIN_MAX, PAST_LEN)
    L = DEPTH
    return {
        'x_prompt': nrm(ks[0], (BATCH, SEQ, D_MODEL), 1.0),
        'x_sample': nrm(ks[1], (DEC_BATCH, DEC_SEQ, D_MODEL), 1.0),
        'cache_win_k': nrm(ks[2], (L, DEC_BATCH, wb, N_HEADS, HEAD_DIM), 1.0),
        'cache_win_v': nrm(ks[3], (L, DEC_BATCH, wb, N_HEADS, HEAD_DIM), 1.0),
        'state_conv_a': nrm(ks[4], (L, DEC_BATCH, CONV_A_WIDTH - 1, D_CONV), 0.5),
        'state_ffn_conv': nrm(ks[5], (L, DEC_BATCH, FFN_CONV_WIDTH - 1, 2 * D_FF), 0.5),
        'c_prompt': nrm(ks[6], (BATCH, D_MODEL), 1.0),
        'c_sample': nrm(ks[7], (DEC_BATCH, D_MODEL), 1.0),
        'norm_mix_g': 1.0 + nrm(ks[8], (L, D_MODEL), 0.02),
        'norm_ffn_g': 1.0 + nrm(ks[9], (L, D_MODEL), 0.02),
        'w_ada': nrm(ks[10], (L, D_MODEL, 6 * D_MODEL), 0.5 * D_MODEL ** -0.5),
        'b_ada': nrm(ks[11], (L, 6 * D_MODEL), 0.02),
        'w_in': nrm(ks[12], (L, D_MODEL, 2 * D_CONV + 3 * D_ATTN), D_MODEL ** -0.5),
        'conv_a_w': nrm(ks[13], (L, CONV_A_WIDTH, D_CONV), CONV_A_WIDTH ** -0.5),
        'conv_a_b': nrm(ks[14], (L, D_CONV), 0.02),
        'ln_a_g': 1.0 + nrm(ks[15], (L, D_CONV), 0.02),
        'ln_a_b': nrm(ks[16], (L, D_CONV), 0.02),
        'q_norm_g': 1.0 + nrm(ks[17], (L, HEAD_DIM), 0.02),
        'k_norm_g': 1.0 + nrm(ks[18], (L, HEAD_DIM), 0.02),
        'w_out': nrm(ks[19], (L, D_MIX, D_MODEL), D_MIX ** -0.5),
        'w_up': nrm(ks[20], (L, D_MODEL, 2 * D_FF), D_MODEL ** -0.5),
        'ffn_conv_w': nrm(ks[21], (L, FFN_CONV_WIDTH, 2 * D_FF), FFN_CONV_WIDTH ** -0.5),
        'ffn_conv_b': nrm(ks[22], (L, 2 * D_FF), 0.02),
        'w_down': nrm(ks[23], (L, D_FF, D_MODEL), D_FF ** -0.5),
    }


def reference(x_prompt, x_sample, cache_win_k, cache_win_v, state_conv_a, state_ffn_conv,
              c_prompt, c_sample, norm_mix_g, norm_ffn_g, w_ada, b_ada, w_in, conv_a_w, conv_a_b,
              ln_a_g, ln_a_b, q_norm_g, k_norm_g, w_out, w_up, ffn_conv_w, ffn_conv_b, w_down):
    B, S = x_prompt.shape[0], x_prompt.shape[1]
    Bd, T = x_sample.shape[0], x_sample.shape[1]
    pos_p = jnp.arange(S)
    pos_s = PAST_LEN + jnp.arange(T)
    wp = min(WIN_MAX, S)
    xp, xs = x_prompt, x_sample
    kp_l, vp_l, ap_l, fp_l = [], [], [], []
    ks_l, vs_l, as_l, fs_l = [], [], [], []
    for l in range(DEPTH):
        p = dict(norm_mix_g=norm_mix_g[l], norm_ffn_g=norm_ffn_g[l], w_ada=w_ada[l], b_ada=b_ada[l],
                 w_in=w_in[l], conv_a_w=conv_a_w[l], conv_a_b=conv_a_b[l], ln_a_g=ln_a_g[l],
                 ln_a_b=ln_a_b[l], q_norm_g=q_norm_g[l], k_norm_g=k_norm_g[l], w_out=w_out[l],
                 w_up=w_up[l], ffn_conv_w=ffn_conv_w[l], ffn_conv_b=ffn_conv_b[l], w_down=w_down[l])
        a0 = jnp.zeros((B, CONV_A_WIDTH - 1, D_CONV), xp.dtype)
        f0 = jnp.zeros((B, FFN_CONV_WIDTH - 1, 2 * D_FF), xp.dtype)
        xp, kp, vp, ap, fp = trunk_layer(xp, c_prompt, pos_p, a0, f0, prompt_attend, p)
        kp_l.append(kp[:, S - wp:])
        vp_l.append(vp[:, S - wp:])
        ap_l.append(ap)
        fp_l.append(fp)
        kb, vb = cache_win_k[l], cache_win_v[l]
        wb = kb.shape[1]
        attend_s = functools.partial(sample_attend, k_buf=kb, v_buf=vb)
        xs, kn, vn, an, fn = trunk_layer(xs, c_sample, pos_s, state_conv_a[l], state_ffn_conv[l], attend_s, p)
        ks_l.append(jnp.concatenate([kb, kn.astype(kb.dtype)], axis=1)[:, T:T + wb])
        vs_l.append(jnp.concatenate([vb, vn.astype(vb.dtype)], axis=1)[:, T:T + wb])
        as_l.append(an)
        fs_l.append(fn)
    return (xp, xs, jnp.stack(kp_l), jnp.stack(vp_l), jnp.stack(ap_l), jnp.stack(fp_l),
            jnp.stack(ks_l), jnp.stack(vs_l), jnp.stack(as_l), jnp.stack(fs_l))
```

```python
import functools

import numpy as np
import jax
import jax.numpy as jnp
from jax import lax
from jax.experimental import pallas as pl
from jax.experimental.pallas import tpu as pltpu

F32 = jnp.float32
BF16 = jnp.bfloat16

D_MODEL = 2048
HEAD_DIM = 64
D_CONV = D_MODEL // 2
N_HEADS = (D_MODEL // 2) // HEAD_DIM
D_ATTN = N_HEADS * HEAD_DIM
CONV_A_WIDTH = 31
DIL_PATTERNS = ((128, 1), (512, 4), (2048, 16))
BAND = 128
WIN_MAX = 2048
ROPE_THETA = 10000.0
D_FF = 5632
FFN_CONV_WIDTH = 3
EPS = 1e-6
NEG_INF = -1e30
PAST_LEN = 16384

LANES = 128
SUBLANES = 8
VMEM_LIMIT = 56 * 1024 * 1024


def _sigmoid(x):
    return 1.0 / (1.0 + jnp.exp(-x))


def _params(sem, vmem=VMEM_LIMIT):
    return pltpu.CompilerParams(dimension_semantics=sem, vmem_limit_bytes=vmem)


def _ada_kernel(c_ref, w_ref, b_ref, o_ref):
    c = c_ref[...]
    s = (c * _sigmoid(c)).astype(BF16)
    o_ref[...] = jnp.dot(s, w_ref[...].astype(BF16), preferred_element_type=F32) + b_ref[...]


def _ada(c_all, w_ada, b_ada):
    rows, d = c_all.shape
    n = w_ada.shape[1]
    tn = 1024
    return pl.pallas_call(
        _ada_kernel,
        grid=(n // tn,),
        in_specs=[pl.BlockSpec((rows, d), lambda j: (0, 0)),
                  pl.BlockSpec((d, tn), lambda j: (0, j)),
                  pl.BlockSpec((1, tn), lambda j: (0, j))],
        out_specs=pl.BlockSpec((rows, tn), lambda j: (0, j)),
        out_shape=jax.ShapeDtypeStruct((rows, n), F32),
        compiler_params=_params(("parallel",)),
        name="ada_mod",
    )(c_all, w_ada, b_ada)


def _head_norm_rope(x, gain, cos, sin, e):
    ms = jnp.dot((x * x).astype(BF16), e, preferred_element_type=F32)
    xn = x * lax.rsqrt(ms + EPS) * gain
    lane = lax.broadcasted_iota(jnp.int32, (1, LANES), 1)
    first_half = jnp.bitwise_and(lane, HEAD_DIM - 1) < (HEAD_DIM // 2)
    out = []
    for c in range(D_ATTN // LANES):
        xc = xn[:, c * LANES:(c + 1) * LANES]
        partner = jnp.where(first_half, pltpu.roll(xc, LANES - HEAD_DIM // 2, 1),
                            pltpu.roll(xc, HEAD_DIM // 2, 1))
        out.append(xc * cos + partner * sin)
    return out


def _inproj_kernel(x_ref, sc_ref, sh_ref, gmix_ref, w_ref, qg_ref, kg_ref, cos_ref, sin_ref, e_ref,
                   g_ref, q_ref, k_ref, v_ref, *tail_refs, nt):
    x = x_ref[0]
    ms = jnp.mean(x * x, axis=-1, keepdims=True)
    h = x * lax.rsqrt(ms + EPS) * gmix_ref[...]
    hb = (h * (1.0 + sc_ref[0]) + sh_ref[0]).astype(BF16)

    def proj(j):
        return jnp.dot(hb, w_ref[:, j * D_CONV:(j + 1) * D_CONV], preferred_element_type=F32)

    g = proj(0) * _sigmoid(proj(1))
    g_ref[0] = g.astype(g_ref.dtype)
    cos = cos_ref[...]
    sin = sin_ref[...]
    e = e_ref[...]
    qc = _head_norm_rope(proj(2), qg_ref[...], cos, sin, e)
    for c, qv in enumerate(qc):
        q_ref[0, :, c * LANES:(c + 1) * LANES] = (qv * (HEAD_DIM ** -0.5)).astype(q_ref.dtype)
    kc = _head_norm_rope(proj(3), kg_ref[...], cos, sin, e)
    for c, kv in enumerate(kc):
        k_ref[0, :, c * LANES:(c + 1) * LANES] = kv.astype(k_ref.dtype)
    v = proj(4)
    v_ref[0] = v.astype(v_ref.dtype)

    if tail_refs:
        gt_ref, kt_ref, vt_ref = tail_refs
        i = pl.program_id(1)
        rows = gt_ref.shape[1]

        @pl.when(i == nt - 1)
        def _():
            gt_ref[0] = g[g.shape[0] - rows:, :]

        @pl.when(i >= nt // 2)
        def _():
            for c, kv in enumerate(kc):
                kt_ref[0, :, c * LANES:(c + 1) * LANES] = kv
            vt_ref[0] = v


def _in_proj(x, sc, sh, gmix, w_in, qg, kg, cos, sin, e, *, tm, out_dtype, tails):
    bsz, seq, d = x.shape
    nt = seq // tm
    lm = sc.shape[1]
    mod_rows = tm if lm == seq else 1

    def mod_map(piece):
        if lm == seq:
            return lambda b, i: (b, i, piece)
        return lambda b, i: (b, 0, piece)

    row_spec = pl.BlockSpec((1, tm, D_CONV), lambda b, i: (b, i, 0))
    in_specs = [
        pl.BlockSpec((1, tm, d), lambda b, i: (b, i, 0)),
        pl.BlockSpec((1, mod_rows, d), mod_map(1)),
        pl.BlockSpec((1, mod_rows, d), mod_map(0)),
        pl.BlockSpec((1, d), lambda b, i: (0, 0)),
        pl.BlockSpec(w_in.shape, lambda b, i: (0, 0)),
        pl.BlockSpec((1, D_ATTN), lambda b, i: (0, 0)),
        pl.BlockSpec((1, D_ATTN), lambda b, i: (0, 0)),
        pl.BlockSpec((tm, LANES), lambda b, i: (i, 0)),
        pl.BlockSpec((tm, LANES), lambda b, i: (i, 0)),
        pl.BlockSpec(e.shape, lambda b, i: (0, 0)),
    ]
    out_specs = [row_spec] * 4
    out_shape = [jax.ShapeDtypeStruct((bsz, seq, D_CONV), out_dtype)] * 4
    if tails:
        half = nt // 2
        tail_map = lambda b, i: (b, jnp.maximum(i - half, 0), 0)
        out_specs = out_specs + [pl.BlockSpec((1, 32, D_CONV), lambda b, i: (b, 0, 0)),
                                 pl.BlockSpec((1, tm, D_ATTN), tail_map),
                                 pl.BlockSpec((1, tm, D_ATTN), tail_map)]
        out_shape = out_shape + [jax.ShapeDtypeStruct((bsz, 32, D_CONV), F32),
                                 jax.ShapeDtypeStruct((bsz, seq // 2, D_ATTN), F32),
                                 jax.ShapeDtypeStruct((bsz, seq // 2, D_ATTN), F32)]
    return pl.pallas_call(
        functools.partial(_inproj_kernel, nt=nt),
        grid=(bsz, nt),
        in_specs=in_specs,
        out_specs=out_specs,
        out_shape=out_shape,
        compiler_params=_params(("arbitrary", "arbitrary")),
        name="in_proj",
    )(x, sc, sh, gmix, w_in, qg, kg, cos, sin, e)


def _conv_kernel(*refs, tc, rs, halo, has_prev, rc):
    if has_prev:
        g_ref, prev_ref, w_ref, b_ref, lg_ref, lb_ref, o_ref, buf = refs
    else:
        g_ref, w_ref, b_ref, lg_ref, lb_ref, o_ref, buf = refs
    i = pl.program_id(1)

    @pl.when(i == 0)
    def _():
        if has_prev:
            buf[0:halo, :] = prev_ref[0]
        else:
            buf[0:halo, :] = jnp.zeros((halo, D_CONV), F32)

    @pl.when(i > 0)
    def _():
        buf[0:halo, :] = buf[tc:tc + halo, :]

    buf[halo:halo + tc, :] = g_ref[0].astype(F32)
    base = halo - (CONV_A_WIDTH - 1) * rs
    bias = b_ref[...]
    lg = lg_ref[...]
    lb = lb_ref[...]
    for c in range(tc // rc):
        r0 = c * rc
        acc = jnp.broadcast_to(bias, (rc, D_CONV))
        for k in range(CONV_A_WIDTH):
            off = r0 + base + k * rs
            acc = acc + w_ref[k:k + 1, :] * buf[off:off + rc, :]
        mu = jnp.mean(acc, axis=-1, keepdims=True)
        xc = acc - mu
        var = jnp.mean(xc * xc, axis=-1, keepdims=True)
        y = xc * lax.rsqrt(var + EPS) * lg + lb
        o_ref[0, r0:r0 + rc, :] = (y * _sigmoid(y)).astype(o_ref.dtype)


def _conv_mix(g, prev, w, b, lg, lb, *, tc, rs, out_dtype):
    bsz, seq, ch = g.shape
    has_prev = prev is not None
    halo = prev.shape[1] if has_prev else 32
    rc = min(tc, 32)
    in_specs = [pl.BlockSpec((1, tc, ch), lambda bb, i: (bb, i, 0))]
    args = [g]
    if has_prev:
        in_specs.append(pl.BlockSpec((1, halo, ch), lambda bb, i: (bb, 0, 0)))
        args.append(prev)
    in_specs += [pl.BlockSpec(w.shape, lambda bb, i: (0, 0))] + [pl.BlockSpec((1, ch), lambda bb, i: (0, 0))] * 3
    args += [w, b, lg, lb]
    return pl.pallas_call(
        functools.partial(_conv_kernel, tc=tc, rs=rs, halo=halo, has_prev=has_prev, rc=rc),
        grid=(bsz, seq // tc),
        in_specs=in_specs,
        out_specs=pl.BlockSpec((1, tc, ch), lambda bb, i: (bb, i, 0)),
        out_shape=jax.ShapeDtypeStruct((bsz, seq, ch), out_dtype),
        scratch_shapes=[pltpu.VMEM((halo + tc, ch), F32)],
        compiler_params=_params(("arbitrary", "arbitrary")),
        name="conv_mix",
    )(*args)


def _band_attn_kernel(q_ref, k_ref, v_ref, o_ref, lse_ref, *, lr, hb):
    nblk = lr // BAND
    qi = lax.broadcasted_iota(jnp.int32, (BAND, 2 * BAND), 0)
    kj = lax.broadcasted_iota(jnp.int32, (BAND, 2 * BAND), 1)
    bias_band = jnp.where(kj >= qi, jnp.where(kj <= qi + BAND, 0.0, NEG_INF), NEG_INF).astype(F32)
    qi1 = lax.broadcasted_iota(jnp.int32, (BAND, BAND), 0)
    kj1 = lax.broadcasted_iota(jnp.int32, (BAND, BAND), 1)
    bias_first = jnp.where(kj1 <= qi1, 0.0, NEG_INF).astype(F32)

    def block(q0, k0, nk, bias):
        for hh in range(hb):
            lo = hh * HEAD_DIM
            qb = q_ref[0, pl.ds(q0, BAND), lo:lo + HEAD_DIM]
            kb = k_ref[0, pl.ds(k0, nk), lo:lo + HEAD_DIM]
            vb = v_ref[0, pl.ds(k0, nk), lo:lo + HEAD_DIM]
            s = lax.dot_general(qb, kb, (((1,), (1,)), ((), ())), preferred_element_type=F32) + bias
            m = jnp.max(s, axis=-1, keepdims=True)
            p = jnp.exp(s - m)
            l = jnp.sum(p, axis=-1, keepdims=True)
            o = jnp.dot(p.astype(BF16), vb, preferred_element_type=F32) * (1.0 / l)
            o_ref[0, pl.ds(q0, BAND), lo:lo + HEAD_DIM] = o.astype(o_ref.dtype)
            lse_ref[0, 0, 0, pl.ds(q0, BAND), hh:hh + 1] = m + jnp.log(l)

    block(0, 0, BAND, bias_first)

    def body(i, carry):
        q0 = pl.multiple_of(i * BAND, BAND)
        k0 = pl.multiple_of((i - 1) * BAND, BAND)
        block(q0, k0, 2 * BAND, bias_band)
        return carry

    lax.fori_loop(1, nblk, body, 0)


def _band_attn(q, k, v, dil, width):
    bsz, seq, da = q.shape
    lr = seq // dil
    hb = width // HEAD_DIM
    ncol = da // width
    view = lambda a: a.reshape(bsz, lr, dil * da)
    spec = pl.BlockSpec((1, lr, width), lambda b, c: (b, 0, c))
    o, lse = pl.pallas_call(
        functools.partial(_band_attn_kernel, lr=lr, hb=hb),
        grid=(bsz, dil * ncol),
        in_specs=[spec, spec, spec],
        out_specs=[spec,
                   pl.BlockSpec((1, 1, 1, lr, hb), lambda b, c: (b, c // ncol, c % ncol, 0, 0))],
        out_shape=[jax.ShapeDtypeStruct((bsz, lr, dil * da), BF16),
                   jax.ShapeDtypeStruct((bsz, dil, ncol, lr, hb), F32)],
        compiler_params=_params(("parallel", "parallel")),
        name=f"band_attn_d{dil}",
    )(view(q), view(k), view(v))
    lse = lse.transpose(0, 3, 1, 2, 4).reshape(bsz, seq, N_HEADS)
    return o.reshape(bsz, seq, da), lse


FAR_GROUPS = (WIN_MAX - 512) // 16
NEAR_ROWS = 512
DEC_T = 4


def _sample_mult_tables():
    col = np.arange(DEC_T * N_HEADS)
    t, hq = col // N_HEADS, col % N_HEADS

    def table(r, h):
        d = WIN_MAX + t[None, :] - r[:, None]
        m = np.zeros(d.shape, np.float32)
        for window, dil in DIL_PATTERNS:
            m += ((d >= 0) & (d <= window) & (d % dil == 0)).astype(np.float32)
        return m * (h[:, None] == hq[None, :])

    idx = np.arange(FAR_GROUPS * DEC_T * N_HEADS)
    far = table(16 * (idx // (DEC_T * N_HEADS)) + (idx // N_HEADS) % DEC_T, idx % N_HEADS)
    idx = np.arange(NEAR_ROWS * N_HEADS)
    near = table(WIN_MAX - NEAR_ROWS + idx // N_HEADS, idx % N_HEADS)
    idx = np.arange(DEC_T * N_HEADS)
    new = table(WIN_MAX + idx // N_HEADS, idx % N_HEADS)
    return far, near, new


def _sattn_kernel(qt_ref, fk_ref, nk_ref, fv_ref, nv_ref, kn_ref, vn_ref, mf_ref, mn_ref, mw_ref,
                  ck_hbm, cv_hbm, o_ref, ok_hbm, ov_hbm, sem):
    b = pl.program_id(0)
    keep = WIN_MAX - DEC_T
    copies = [
        pltpu.make_async_copy(ck_hbm.at[0, b, pl.ds(DEC_T, keep)], ok_hbm.at[0, b, pl.ds(0, keep)], sem.at[0]),
        pltpu.make_async_copy(cv_hbm.at[0, b, pl.ds(DEC_T, keep)], ov_hbm.at[0, b, pl.ds(0, keep)], sem.at[1]),
        pltpu.make_async_copy(kn_ref.at[0], ok_hbm.at[0, b, pl.ds(keep, DEC_T)], sem.at[2]),
        pltpu.make_async_copy(vn_ref.at[0], ov_hbm.at[0, b, pl.ds(keep, DEC_T)], sem.at[3]),
    ]
    for cp in copies:
        cp.start()

    qt = qt_ref[0]
    keys = [fk_ref[0].reshape(-1, HEAD_DIM), nk_ref[0].reshape(-1, HEAD_DIM), kn_ref[0].reshape(-1, HEAD_DIM)]
    vals = [fv_ref[0].reshape(-1, HEAD_DIM), nv_ref[0].reshape(-1, HEAD_DIM), vn_ref[0].reshape(-1, HEAD_DIM)]
    mults = [mf_ref[...], mn_ref[...], mw_ref[...]]
    scores = []
    m = None
    for kk, mu in zip(keys, mults):
        s = jnp.dot(kk.astype(BF16), qt, preferred_element_type=F32)
        s = jnp.where(mu > 0.0, s, NEG_INF)
        scores.append(s)
        sm = jnp.max(s, axis=0, keepdims=True)
        m = sm if m is None else jnp.maximum(m, sm)
    acc = jnp.zeros((DEC_T * N_HEADS, HEAD_DIM), F32)
    den = jnp.zeros((DEC_T * N_HEADS, LANES), F32)
    for s, mu, vv in zip(scores, mults, vals):
        p = (mu * jnp.exp(s - m)).astype(BF16)
        contract_rows = (((0,), (0,)), ((), ()))
        acc = acc + lax.dot_general(p, vv.astype(BF16), contract_rows, preferred_element_type=F32)
        den = den + lax.dot_general(p, jnp.ones((p.shape[0], LANES), BF16), contract_rows,
                                    preferred_element_type=F32)
    o_ref[0] = acc * (1.0 / den[:, :HEAD_DIM])

    for cp in copies:
        cp.wait()


def _sample_attn(qt, cache_k, cache_v, k_new, v_new):
    bd = qt.shape[0]
    far_view = lambda c: c.reshape(bd, WIN_MAX // 16, 16, N_HEADS, HEAD_DIM)
    far_spec = pl.BlockSpec((1, FAR_GROUPS, DEC_T, N_HEADS, HEAD_DIM), lambda b: (b, 0, 0, 0, 0))
    near_spec = pl.BlockSpec((1, NEAR_ROWS, N_HEADS, HEAD_DIM),
                             lambda b: (b, WIN_MAX // NEAR_ROWS - 1, 0, 0))
    new_spec = pl.BlockSpec((1, DEC_T, N_HEADS, HEAD_DIM), lambda b: (b, 0, 0, 0))
    sq = DEC_T * N_HEADS
    far, near, new = (jnp.asarray(t) for t in _sample_mult_tables())
    const = lambda a: pl.BlockSpec(a.shape, lambda b: (0, 0))
    any_spec = pl.BlockSpec(memory_space=pl.ANY)
    return pl.pallas_call(
        _sattn_kernel,
        grid=(bd,),
        in_specs=[pl.BlockSpec((1, HEAD_DIM, sq), lambda b: (b, 0, 0)),
                  far_spec, near_spec, far_spec, near_spec, new_spec, new_spec,
                  const(far), const(near), const(new), any_spec, any_spec],
        out_specs=[pl.BlockSpec((1, sq, HEAD_DIM), lambda b: (b, 0, 0)), any_spec, any_spec],
        out_shape=[jax.ShapeDtypeStruct((bd, sq, HEAD_DIM), F32),
                   jax.ShapeDtypeStruct(cache_k.shape, cache_k.dtype),
                   jax.ShapeDtypeStruct(cache_v.shape, cache_v.dtype)],
        scratch_shapes=[pltpu.SemaphoreType.DMA((4,))],
        compiler_params=_params(("arbitrary",)),
        name="sample_attn",
    )(qt, far_view(cache_k[0]), cache_k[0], far_view(cache_v[0]), cache_v[0], k_new, v_new,
      far, near, new, cache_k, cache_v)


def _outproj_kernel(*refs, n_pat):
    a_ref = refs[0]
    o_refs = refs[1:1 + n_pat]
    pos = 1 + n_pat
    lse_refs = ()
    if n_pat > 1:
        lse_refs = refs[pos:pos + n_pat]
        pos += n_pat
    x_ref, g1_ref, sc_ref, sh_ref, gn_ref, w_ref, e_ref, x1_ref, h2_ref = refs[pos:]

    if n_pat > 1:
        lses = [r[0] for r in lse_refs]
        mx = functools.reduce(jnp.maximum, lses)
        es = [jnp.exp(l - mx) for l in lses]
        inv = 1.0 / functools.reduce(lambda a, c: a + c, es)
        e16 = e_ref[...]
        o = None
        for ep, o_ref in zip(es, o_refs):
            wp = ep * inv
            hi = wp.astype(BF16)
            lo = (wp - hi.astype(F32)).astype(BF16)
            wb = (jnp.dot(hi, e16, preferred_element_type=F32)
                  + jnp.dot(lo, e16, preferred_element_type=F32))
            term = wb * o_ref[0].astype(F32)
            o = term if o is None else o + term
        ob = o.astype(BF16)
    else:
        ob = o_refs[0][0].astype(BF16)

    mix = (jnp.dot(a_ref[0].astype(BF16), w_ref[0:D_CONV, :], preferred_element_type=F32)
           + jnp.dot(ob, w_ref[D_CONV:, :], preferred_element_type=F32))
    x1 = x_ref[0] + g1_ref[0] * mix
    x1_ref[0] = x1
    ms = jnp.mean(x1 * x1, axis=-1, keepdims=True)
    h2 = x1 * lax.rsqrt(ms + EPS) * gn_ref[...]
    h2_ref[0] = (h2 * (1.0 + sc_ref[0]) + sh_ref[0]).astype(h2_ref.dtype)


def _out_proj(a_out, outs, lses, x, mod, gn, w_out, e16, *, tm):
    bsz, seq, d = x.shape
    n_pat = len(outs)
    lm = mod.shape[1]
    mod_rows = tm if lm == seq else 1

    def mod_map(piece):
        if lm == seq:
            return lambda b, i: (b, i, piece)
        return lambda b, i: (b, 0, piece)

    half_spec = pl.BlockSpec((1, tm, D_CONV), lambda b, i: (b, i, 0))
    full_spec = pl.BlockSpec((1, tm, d), lambda b, i: (b, i, 0))
    in_specs = [half_spec] * (1 + n_pat)
    args = [a_out, *outs]
    if n_pat > 1:
        in_specs += [pl.BlockSpec((1, tm, N_HEADS), lambda b, i: (b, i, 0))] * n_pat
        args += list(lses)
    in_specs += [full_spec,
                 pl.BlockSpec((1, mod_rows, d), mod_map(2)),
                 pl.BlockSpec((1, mod_rows, d), mod_map(4)),
                 pl.BlockSpec((1, mod_rows, d), mod_map(3)),
                 pl.BlockSpec((1, d), lambda b, i: (0, 0)),
                 pl.BlockSpec(w_out.shape, lambda b, i: (0, 0)),
                 pl.BlockSpec(e16.shape, lambda b, i: (0, 0))]
    args += [x, mod, mod, mod, gn, w_out, e16]
    return pl.pallas_call(
        functools.partial(_outproj_kernel, n_pat=n_pat),
        grid=(bsz, seq // tm),
        in_specs=in_specs,
        out_specs=[full_spec, full_spec],
        out_shape=[jax.ShapeDtypeStruct((bsz, seq, d), F32), jax.ShapeDtypeStruct((bsz, seq, d), BF16)],
        compiler_params=_params(("parallel", "parallel")),
        name="out_proj",
    )(*args)


def _ffn_kernel(*refs, tm, rs, halo, has_prev, nf):
    (h2_ref, x1_ref, g2_ref, wg_ref, wv_ref, wd_ref, cwg_ref, cwv_ref, cbg_ref, cbv_ref) = refs[:10]
    pos = 10
    prev_refs = (None, None)
    if has_prev:
        prev_refs = refs[pos:pos + 2]
        pos += 2
    y_ref, tg_ref, tv_ref, acc_ref, ug_buf, uv_buf, carry_ref = refs[pos:]
    m = pl.program_id(1)
    f = pl.program_id(2)
    h2 = h2_ref[0]

    def conv_half(w_ref, cw_ref, cb_ref, prev_ref, ubuf, t_ref, slot):
        u = jnp.dot(h2, w_ref[...], preferred_element_type=F32)

        @pl.when(m == 0)
        def _():
            if has_prev:
                ubuf[0:halo, :] = prev_ref[0]
            else:
                ubuf[0:halo, :] = jnp.zeros((halo, u.shape[1]), F32)

        @pl.when(m > 0)
        def _():
            ubuf[0:halo, :] = carry_ref[f, slot]

        ubuf[halo:halo + tm, :] = u
        last = u[tm - halo:, :]
        carry_ref[f, slot] = last
        t_ref[0] = last
        return (cb_ref[...] + cw_ref[0:1, :] * ubuf[halo - 2 * rs:halo - 2 * rs + tm, :]
                + cw_ref[1:2, :] * ubuf[halo - rs:halo - rs + tm, :] + cw_ref[2:3, :] * u)

    gate = conv_half(wg_ref, cwg_ref, cbg_ref, prev_refs[0], ug_buf, tg_ref, 0)
    val = conv_half(wv_ref, cwv_ref, cbv_ref, prev_refs[1], uv_buf, tv_ref, 1)
    act = (gate * _sigmoid(gate) * val).astype(BF16)
    contrib = jnp.dot(act, wd_ref[...], preferred_element_type=F32)

    @pl.when(f == 0)
    def _():
        acc_ref[...] = contrib

    @pl.when(f > 0)
    def _():
        acc_ref[...] += contrib

    @pl.when(f == nf - 1)
    def _():
        y_ref[0] = x1_ref[0] + g2_ref[0] * acc_ref[...]


def _conv_ffn(h2, x1, mod, w_up, w_down, cw, cb, prev, *, tm, tf, rs):
    bsz, seq, d = x1.shape
    nf = D_FF // tf
    has_prev = prev is not None
    halo = 2 * rs if has_prev else SUBLANES
    lm = mod.shape[1]
    mod_rows = tm if lm == seq else 1
    g2_map = (lambda b, m, f: (b, m, 5)) if lm == seq else (lambda b, m, f: (b, 0, 5))
    row = lambda b, m, f: (b, m, 0)
    gate_col = lambda b, m, f: (0, f)
    val_col = lambda b, m, f: (0, nf + f)
    in_specs = [pl.BlockSpec((1, tm, d), row), pl.BlockSpec((1, tm, d), row),
                pl.BlockSpec((1, mod_rows, d), g2_map),
                pl.BlockSpec((d, tf), gate_col), pl.BlockSpec((d, tf), val_col),
                pl.BlockSpec((tf, d), lambda b, m, f: (f, 0)),
                pl.BlockSpec((FFN_CONV_WIDTH, tf), gate_col), pl.BlockSpec((FFN_CONV_WIDTH, tf), val_col),
                pl.BlockSpec((1, tf), gate_col), pl.BlockSpec((1, tf), val_col)]
    args = [h2, x1, mod, w_up, w_up, w_down, cw, cw, cb, cb]
    if has_prev:
        in_specs += [pl.BlockSpec((1, halo, tf), lambda b, m, f: (b, 0, f)),
                     pl.BlockSpec((1, halo, tf), lambda b, m, f: (b, 0, nf + f))]
        args += [prev, prev]
    nm = seq // tm
    tail_spec = pl.BlockSpec((1, halo, tf), lambda b, m, f: (b, m, f))
    tail_shape = jax.ShapeDtypeStruct((bsz, nm * halo, D_FF), F32)
    y, tail_g, tail_v = pl.pallas_call(
        functools.partial(_ffn_kernel, tm=tm, rs=rs, halo=halo, has_prev=has_prev, nf=nf),
        grid=(bsz, nm, nf),
        in_specs=in_specs,
        out_specs=[pl.BlockSpec((1, tm, d), row), tail_spec, tail_spec],
        out_shape=[jax.ShapeDtypeStruct((bsz, seq, d), F32), tail_shape, tail_shape],
        scratch_shapes=[pltpu.VMEM((tm, d), F32),
                        pltpu.VMEM((halo + tm, tf), F32), pltpu.VMEM((halo + tm, tf), F32),
                        pltpu.VMEM((nf, 2, halo, tf), F32)],
        compiler_params=_params(("arbitrary", "arbitrary", "arbitrary")),
        name="conv_ffn",
    )(*args)
    last = (nm - 1) * halo
    return y, jnp.concatenate([tail_g[:, last:], tail_v[:, last:]], axis=-1)


def _rope_tables(pos):
    half = HEAD_DIM // 2
    inv = ROPE_THETA ** (-jnp.arange(half, dtype=F32) / half)
    ang = pos.astype(F32)[:, None] * inv[None, :]
    cos = jnp.cos(ang)
    sin = jnp.sin(ang)
    cos_l = jnp.concatenate([cos, cos, cos, cos], axis=-1)
    sin_l = jnp.concatenate([-sin, sin, -sin, sin], axis=-1)
    return cos_l, sin_l


def _head_mean_matrix():
    h = np.arange(D_ATTN) // HEAD_DIM
    return jnp.asarray((h[:, None] == h[None, :]).astype(np.float32) / HEAD_DIM, dtype=BF16)


def _head_expand_matrix():
    h = np.arange(D_ATTN) // HEAD_DIM
    return jnp.asarray((np.arange(N_HEADS)[:, None] == h[None, :]).astype(np.float32), dtype=BF16)


def kernel(x_prompt, x_sample, cache_win_k, cache_win_v, state_conv_a, state_ffn_conv, c_prompt, c_sample,
           norm_mix_g, norm_ffn_g, w_ada, b_ada, w_in, conv_a_w, conv_a_b, ln_a_g, ln_a_b, q_norm_g, k_norm_g,
           w_out, w_up, ffn_conv_w, ffn_conv_b, w_down):
    bsz, seq, d = x_prompt.shape
    bd, dt, _ = x_sample.shape
    assert w_ada.shape[0] == 1 and dt == DEC_T and cache_win_k.shape[2] == WIN_MAX

    w_in_b = w_in[0].astype(BF16)
    w_out_b = w_out[0].astype(BF16)
    w_up_b = w_up[0].astype(BF16)
    w_down_b = w_down[0].astype(BF16)
    e_mean = _head_mean_matrix()
    e16 = _head_expand_matrix()
    qg = jnp.tile(q_norm_g[0], N_HEADS)[None, :]
    kg = jnp.tile(k_norm_g[0], N_HEADS)[None, :]

    c_all = jnp.concatenate([c_prompt, c_sample, jnp.zeros((16 - bsz - bd, d), F32)], axis=0)
    mod = _ada(c_all, w_ada[0], b_ada)
    mod_p = mod[:bsz][:, None, :]
    mod_s = jnp.tile(mod[bsz:bsz + bd], (dt, 1))[None]

    cos_p, sin_p = _rope_tables(jnp.arange(seq))
    g_p, q_p, k_p, v_p, g_tail, k_tail, v_tail = _in_proj(
        x_prompt, mod_p, mod_p, norm_mix_g, w_in_b, qg, kg, cos_p, sin_p, e_mean,
        tm=256, out_dtype=BF16, tails=True)
    a_p = _conv_mix(g_p, None, conv_a_w[0], conv_a_b, ln_a_g, ln_a_b, tc=256, rs=1, out_dtype=BF16)
    outs, lses = [], []
    for (window, dil), width in zip(DIL_PATTERNS, (128, 512, 1024)):
        o, lse = _band_attn(q_p, k_p, v_p, dil, width)
        outs.append(o)
        lses.append(lse)
    x1_p, h2_p = _out_proj(a_p, outs, lses, x_prompt, mod_p, norm_ffn_g, w_out_b, e16, tm=256)
    y_p, ftail_p = _conv_ffn(h2_p, x1_p, mod_p, w_up_b, w_down_b, ffn_conv_w[0], ffn_conv_b, None,
                             tm=512, tf=512, rs=1)

    rows = dt * bd
    to_tm = lambda a: a.transpose(1, 0, 2).reshape(1, a.shape[0] * a.shape[1], a.shape[2])
    xs = to_tm(x_sample)
    cos_s, sin_s = _rope_tables(PAST_LEN + jnp.arange(rows) // bd)
    g_s, q_s, k_s, v_s = _in_proj(xs, mod_s, mod_s, norm_mix_g, w_in_b, qg, kg, cos_s, sin_s, e_mean,
                                  tm=rows, out_dtype=F32, tails=False)
    a_s = _conv_mix(g_s, to_tm(state_conv_a[0]), conv_a_w[0], conv_a_b, ln_a_g, ln_a_b,
                    tc=rows, rs=bd, out_dtype=BF16)
    heads = lambda a: a.reshape(dt, bd, N_HEADS, HEAD_DIM)
    qt = heads(q_s).transpose(1, 3, 0, 2).reshape(bd, HEAD_DIM, dt * N_HEADS).astype(BF16)
    k_new = heads(k_s).transpose(1, 0, 2, 3)
    v_new = heads(v_s).transpose(1, 0, 2, 3)
    o_s, win_k_s, win_v_s = _sample_attn(qt, cache_win_k, cache_win_v, k_new, v_new)
    o_s = o_s.reshape(bd, dt, N_HEADS * HEAD_DIM).transpose(1, 0, 2).reshape(1, rows, D_ATTN)
    x1_s, h2_s = _out_proj(a_s, [o_s], None, xs, mod_s, norm_ffn_g, w_out_b, e16, tm=rows)
    y_s, ftail_s = _conv_ffn(h2_s, x1_s, mod_s, w_up_b, w_down_b, ffn_conv_w[0], ffn_conv_b,
                             to_tm(state_ffn_conv[0]), tm=rows, tf=512, rs=bd)

    from_tm = lambda a, t: a.reshape(t, bd, a.shape[-1]).transpose(1, 0, 2)
    keep = min(WIN_MAX, seq)
    out_heads = lambda a: a.reshape(1, bsz, keep, N_HEADS, HEAD_DIM)
    conv_a_s = jnp.concatenate([state_conv_a[0], from_tm(g_s[0], dt)], axis=1)[:, dt:]
    return (y_p, from_tm(y_s[0], dt),
            out_heads(k_tail), out_heads(v_tail),
            g_tail[:, 32 - (CONV_A_WIDTH - 1):][None],
            ftail_p[:, SUBLANES - (FFN_CONV_WIDTH - 1):][None],
            win_k_s, win_v_s,
            conv_a_s[None],
            from_tm(ftail_s[0], FFN_CONV_WIDTH - 1)[None])
```

```python
import functools

import numpy as np
import jax
import jax.numpy as jnp
from jax import lax
from jax.experimental import pallas as pl
from jax.experimental.pallas import tpu as pltpu

F32 = jnp.float32
BF16 = jnp.bfloat16

D_MODEL = 2048
HEAD_DIM = 64
D_CONV = D_MODEL // 2
N_HEADS = (D_MODEL // 2) // HEAD_DIM
D_ATTN = N_HEADS * HEAD_DIM
CONV_A_WIDTH = 31
DIL_PATTERNS = ((128, 1), (512, 4), (2048, 16))
BAND = 128
WIN_MAX = 2048
ROPE_THETA = 10000.0
D_FF = 5632
FFN_CONV_WIDTH = 3
EPS = 1e-6
NEG_INF = -1e30
PAST_LEN = 16384

LANES = 128
SUBLANES = 8
VMEM_LIMIT = 56 * 1024 * 1024


def _sigmoid(x):
    return 1.0 / (1.0 + jnp.exp(-x))


def _params(sem, vmem=VMEM_LIMIT):
    return pltpu.CompilerParams(dimension_semantics=sem, vmem_limit_bytes=vmem)


def _ada_kernel(c_ref, w_ref, b_ref, o_ref):
    c = c_ref[...]
    s = (c * _sigmoid(c)).astype(BF16)
    o_ref[...] = jnp.dot(s, w_ref[...].astype(BF16), preferred_element_type=F32) + b_ref[...]


def _ada(c_all, w_ada, b_ada):
    rows, d = c_all.shape
    n = w_ada.shape[1]
    tn = 1024
    return pl.pallas_call(
        _ada_kernel,
        grid=(n // tn,),
        in_specs=[pl.BlockSpec((rows, d), lambda j: (0, 0)),
                  pl.BlockSpec((d, tn), lambda j: (0, j)),
                  pl.BlockSpec((1, tn), lambda j: (0, j))],
        out_specs=pl.BlockSpec((rows, tn), lambda j: (0, j)),
        out_shape=jax.ShapeDtypeStruct((rows, n), F32),
        compiler_params=_params(("parallel",)),
        name="ada_mod",
    )(c_all, w_ada, b_ada)


def _head_norm_rope(x, gain, cos, sin, e):
    ms = jnp.dot((x * x).astype(BF16), e, preferred_element_type=F32)
    xn = x * lax.rsqrt(ms + EPS) * gain
    lane = lax.broadcasted_iota(jnp.int32, (1, LANES), 1)
    first_half = jnp.bitwise_and(lane, HEAD_DIM - 1) < (HEAD_DIM // 2)
    out = []
    for c in range(D_ATTN // LANES):
        xc = xn[:, c * LANES:(c + 1) * LANES]
        partner = jnp.where(first_half, pltpu.roll(xc, LANES - HEAD_DIM // 2, 1),
                            pltpu.roll(xc, HEAD_DIM // 2, 1))
        out.append(xc * cos + partner * sin)
    return out


def _inproj_kernel(x_ref, sc_ref, sh_ref, gmix_ref, w_ref, qg_ref, kg_ref, cos_ref, sin_ref, e_ref,
                   g_ref, q_ref, k_ref, v_ref, *tail_refs, nt):
    x = x_ref[0]
    ms = jnp.mean(x * x, axis=-1, keepdims=True)
    h = x * lax.rsqrt(ms + EPS) * gmix_ref[...]
    hb = (h * (1.0 + sc_ref[0]) + sh_ref[0]).astype(BF16)

    def proj(j):
        return jnp.dot(hb, w_ref[:, j * D_CONV:(j + 1) * D_CONV], preferred_element_type=F32)

    g = proj(0) * _sigmoid(proj(1))
    g_ref[0] = g.astype(g_ref.dtype)
    cos = cos_ref[...]
    sin = sin_ref[...]
    e = e_ref[...]
    qc = _head_norm_rope(proj(2), qg_ref[...], cos, sin, e)
    for c, qv in enumerate(qc):
        q_ref[0, :, c * LANES:(c + 1) * LANES] = (qv * (HEAD_DIM ** -0.5)).astype(q_ref.dtype)
    kc = _head_norm_rope(proj(3), kg_ref[...], cos, sin, e)
    for c, kv in enumerate(kc):
        k_ref[0, :, c * LANES:(c + 1) * LANES] = kv.astype(k_ref.dtype)
    v = proj(4)
    v_ref[0] = v.astype(v_ref.dtype)

    if tail_refs:
        gt_ref, kt_ref, vt_ref = tail_refs
        i = pl.program_id(1)
        rows = gt_ref.shape[1]

        @pl.when(i == nt - 1)
        def _():
            gt_ref[0] = g[g.shape[0] - rows:, :]

        @pl.when(i >= nt // 2)
        def _():
            for c, kv in enumerate(kc):
                kt_ref[0, :, c * LANES:(c + 1) * LANES] = kv
            vt_ref[0] = v


def _in_proj(x, sc, sh, gmix, w_in, qg, kg, cos, sin, e, *, tm, out_dtype, tails):
    bsz, seq, d = x.shape
    nt = seq // tm
    lm = sc.shape[1]
    mod_rows = tm if lm == seq else 1

    def mod_map(piece):
        if lm == seq:
            return lambda b, i: (b, i, piece)
        return lambda b, i: (b, 0, piece)

    row_spec = pl.BlockSpec((1, tm, D_CONV), lambda b, i: (b, i, 0))
    in_specs = [
        pl.BlockSpec((1, tm, d), lambda b, i: (b, i, 0)),
        pl.BlockSpec((1, mod_rows, d), mod_map(1)),
        pl.BlockSpec((1, mod_rows, d), mod_map(0)),
        pl.BlockSpec((1, d), lambda b, i: (0, 0)),
        pl.BlockSpec(w_in.shape, lambda b, i: (0, 0)),
        pl.BlockSpec((1, D_ATTN), lambda b, i: (0, 0)),
        pl.BlockSpec((1, D_ATTN), lambda b, i: (0, 0)),
        pl.BlockSpec((tm, LANES), lambda b, i: (i, 0)),
        pl.BlockSpec((tm, LANES), lambda b, i: (i, 0)),
        pl.BlockSpec(e.shape, lambda b, i: (0, 0)),
    ]
    out_specs = [row_spec] * 4
    out_shape = [jax.ShapeDtypeStruct((bsz, seq, D_CONV), out_dtype)] * 4
    if tails:
        half = nt // 2
        tail_map = lambda b, i: (b, jnp.maximum(i - half, 0), 0)
        out_specs = out_specs + [pl.BlockSpec((1, 32, D_CONV), lambda b, i: (b, 0, 0)),
                                 pl.BlockSpec((1, tm, D_ATTN), tail_map),
                                 pl.BlockSpec((1, tm, D_ATTN), tail_map)]
        out_shape = out_shape + [jax.ShapeDtypeStruct((bsz, 32, D_CONV), F32),
                                 jax.ShapeDtypeStruct((bsz, seq // 2, D_ATTN), F32),
                                 jax.ShapeDtypeStruct((bsz, seq // 2, D_ATTN), F32)]
    return pl.pallas_call(
        functools.partial(_inproj_kernel, nt=nt),
        grid=(bsz, nt),
        in_specs=in_specs,
        out_specs=out_specs,
        out_shape=out_shape,
        compiler_params=_params(("arbitrary", "arbitrary")),
        name="in_proj",
    )(x, sc, sh, gmix, w_in, qg, kg, cos, sin, e)


def _conv_kernel(*refs, tc, rs, halo, has_prev, rc):
    if has_prev:
        g_ref, prev_ref, w_ref, b_ref, lg_ref, lb_ref, o_ref, buf = refs
    else:
        g_ref, w_ref, b_ref, lg_ref, lb_ref, o_ref, buf = refs
    i = pl.program_id(1)

    @pl.when(i == 0)
    def _():
        if has_prev:
            buf[0:halo, :] = prev_ref[0]
        else:
            buf[0:halo, :] = jnp.zeros((halo, D_CONV), F32)

    @pl.when(i > 0)
    def _():
        buf[0:halo, :] = buf[tc:tc + halo, :]

    buf[halo:halo + tc, :] = g_ref[0].astype(F32)
    base = halo - (CONV_A_WIDTH - 1) * rs
    bias = b_ref[...]
    lg = lg_ref[...]
    lb = lb_ref[...]
    for c in range(tc // rc):
        r0 = c * rc
        acc = jnp.broadcast_to(bias, (rc, D_CONV))
        for k in range(CONV_A_WIDTH):
            off = r0 + base + k * rs
            acc = acc + w_ref[k:k + 1, :] * buf[off:off + rc, :]
        mu = jnp.mean(acc, axis=-1, keepdims=True)
        xc = acc - mu
        var = jnp.mean(xc * xc, axis=-1, keepdims=True)
        y = xc * lax.rsqrt(var + EPS) * lg + lb
        o_ref[0, r0:r0 + rc, :] = (y * _sigmoid(y)).astype(o_ref.dtype)


def _conv_mix(g, prev, w, b, lg, lb, *, tc, rs, out_dtype):
    bsz, seq, ch = g.shape
    has_prev = prev is not None
    halo = prev.shape[1] if has_prev else 32
    rc = min(tc, 32)
    in_specs = [pl.BlockSpec((1, tc, ch), lambda bb, i: (bb, i, 0))]
    args = [g]
    if has_prev:
        in_specs.append(pl.BlockSpec((1, halo, ch), lambda bb, i: (bb, 0, 0)))
        args.append(prev)
    in_specs += [pl.BlockSpec(w.shape, lambda bb, i: (0, 0))] + [pl.BlockSpec((1, ch), lambda bb, i: (0, 0))] * 3
    args += [w, b, lg, lb]
    return pl.pallas_call(
        functools.partial(_conv_kernel, tc=tc, rs=rs, halo=halo, has_prev=has_prev, rc=rc),
        grid=(bsz, seq // tc),
        in_specs=in_specs,
        out_specs=pl.BlockSpec((1, tc, ch), lambda bb, i: (bb, i, 0)),
        out_shape=jax.ShapeDtypeStruct((bsz, seq, ch), out_dtype),
        scratch_shapes=[pltpu.VMEM((halo + tc, ch), F32)],
        compiler_params=_params(("arbitrary", "arbitrary")),
        name="conv_mix",
    )(*args)


def _band_attn_kernel(q_ref, k_ref, v_ref, o_ref, lse_ref, *, lr, hb):
    nblk = lr // BAND
    qi = lax.broadcasted_iota(jnp.int32, (BAND, 2 * BAND), 0)
    kj = lax.broadcasted_iota(jnp.int32, (BAND, 2 * BAND), 1)
    bias_band = jnp.where(kj >= qi, jnp.where(kj <= qi + BAND, 0.0, NEG_INF), NEG_INF).astype(F32)
    qi1 = lax.broadcasted_iota(jnp.int32, (BAND, BAND), 0)
    kj1 = lax.broadcasted_iota(jnp.int32, (BAND, BAND), 1)
    bias_first = jnp.where(kj1 <= qi1, 0.0, NEG_INF).astype(F32)

    def block(q0, k0, nk, bias):
        for hh in range(hb):
            lo = hh * HEAD_DIM
            qb = q_ref[0, pl.ds(q0, BAND), lo:lo + HEAD_DIM]
            kb = k_ref[0, pl.ds(k0, nk), lo:lo + HEAD_DIM]
            vb = v_ref[0, pl.ds(k0, nk), lo:lo + HEAD_DIM]
            s = lax.dot_general(qb, kb, (((1,), (1,)), ((), ())), preferred_element_type=F32) + bias
            m = jnp.max(s, axis=-1, keepdims=True)
            p = jnp.exp(s - m)
            l = jnp.sum(p, axis=-1, keepdims=True)
            o = jnp.dot(p.astype(BF16), vb, preferred_element_type=F32) * (1.0 / l)
            o_ref[0, pl.ds(q0, BAND), lo:lo + HEAD_DIM] = o.astype(o_ref.dtype)
            lse_ref[0, 0, 0, pl.ds(q0, BAND), hh:hh + 1] = m + jnp.log(l)

    block(0, 0, BAND, bias_first)

    def body(i, carry):
        q0 = pl.multiple_of(i * BAND, BAND)
        k0 = pl.multiple_of((i - 1) * BAND, BAND)
        block(q0, k0, 2 * BAND, bias_band)
        return carry

    lax.fori_loop(1, nblk, body, 0)


def _band_attn(q, k, v, dil, width):
    bsz, seq, da = q.shape
    lr = seq // dil
    hb = width // HEAD_DIM
    ncol = da // width
    view = lambda a: a.reshape(bsz, lr, dil * da)
    spec = pl.BlockSpec((1, lr, width), lambda b, c: (b, 0, c))
    o, lse = pl.pallas_call(
        functools.partial(_band_attn_kernel, lr=lr, hb=hb),
        grid=(bsz, dil * ncol),
        in_specs=[spec, spec, spec],
        out_specs=[spec,
                   pl.BlockSpec((1, 1, 1, lr, hb), lambda b, c: (b, c // ncol, c % ncol, 0, 0))],
        out_shape=[jax.ShapeDtypeStruct((bsz, lr, dil * da), BF16),
                   jax.ShapeDtypeStruct((bsz, dil, ncol, lr, hb), F32)],
        compiler_params=_params(("parallel", "parallel")),
        name=f"band_attn_d{dil}",
    )(view(q), view(k), view(v))
    lse = lse.transpose(0, 3, 1, 2, 4).reshape(bsz, seq, N_HEADS)
    return o.reshape(bsz, seq, da), lse


FAR_GROUPS = (WIN_MAX - 512) // 16
NEAR_ROWS = 512
DEC_T = 4


def _sample_mult_tables():
    col = np.arange(DEC_T * N_HEADS)
    t, hq = col // N_HEADS, col % N_HEADS

    def table(r, h):
        d = WIN_MAX + t[None, :] - r[:, None]
        m = np.zeros(d.shape, np.float32)
        for window, dil in DIL_PATTERNS:
            m += ((d >= 0) & (d <= window) & (d % dil == 0)).astype(np.float32)
        return m * (h[:, None] == hq[None, :])

    idx = np.arange(FAR_GROUPS * DEC_T * N_HEADS)
    far = table(16 * (idx // (DEC_T * N_HEADS)) + (idx // N_HEADS) % DEC_T, idx % N_HEADS)
    idx = np.arange(NEAR_ROWS * N_HEADS)
    near = table(WIN_MAX - NEAR_ROWS + idx // N_HEADS, idx % N_HEADS)
    idx = np.arange(DEC_T * N_HEADS)
    new = table(WIN_MAX + idx // N_HEADS, idx % N_HEADS)
    return far, near, new


N_CHUNKS = WIN_MAX // NEAR_ROWS
GROUPS_PER_CHUNK = NEAR_ROWS // 16


def _sattn_kernel(qt_ref, ck_ref, ckn_ref, cv_ref, cvn_ref, kn_ref, vn_ref, mf_ref, mn_ref, mw_ref,
                  o_ref, ok_ref, ov_ref, fk_ref, fv_ref):
    c = pl.program_id(1)
    keep = NEAR_ROWS - DEC_T

    ok_ref[0:keep] = ck_ref[DEC_T:NEAR_ROWS]
    ov_ref[0:keep] = cv_ref[DEC_T:NEAR_ROWS]

    @pl.when(c < N_CHUNKS - 1)
    def _():
        ok_ref[keep:NEAR_ROWS] = ckn_ref[0:DEC_T]
        ov_ref[keep:NEAR_ROWS] = cvn_ref[0:DEC_T]
        g0 = c * GROUPS_PER_CHUNK
        for src, dst in ((ck_ref, fk_ref), (cv_ref, fv_ref)):
            grouped = src[...].reshape(GROUPS_PER_CHUNK, 16, N_HEADS, HEAD_DIM)
            dst[pl.ds(g0, GROUPS_PER_CHUNK)] = grouped[:, 0:DEC_T]

    @pl.when(c == N_CHUNKS - 1)
    def _():
        ok_ref[keep:NEAR_ROWS] = kn_ref[...]
        ov_ref[keep:NEAR_ROWS] = vn_ref[...]
        qt = qt_ref[0]
        flat = lambda r: r[...].reshape(-1, HEAD_DIM)
        keys = [flat(fk_ref), flat(ck_ref), flat(kn_ref)]
        vals = [flat(fv_ref), flat(cv_ref), flat(vn_ref)]
        mults = [mf_ref[...], mn_ref[...], mw_ref[...]]
        scores = []
        m = None
        for kk, mu in zip(keys, mults):
            s = jnp.dot(kk.astype(BF16), qt, preferred_element_type=F32)
            s = jnp.where(mu > 0.0, s, NEG_INF)
            scores.append(s)
            sm = jnp.max(s, axis=0, keepdims=True)
            m = sm if m is None else jnp.maximum(m, sm)
        acc = jnp.zeros((DEC_T * N_HEADS, HEAD_DIM), F32)
        den = jnp.zeros((DEC_T * N_HEADS, LANES), F32)
        contract_rows = (((0,), (0,)), ((), ()))
        for s, mu, vv in zip(scores, mults, vals):
            p = (mu * jnp.exp(s - m)).astype(BF16)
            acc = acc + lax.dot_general(p, vv.astype(BF16), contract_rows, preferred_element_type=F32)
            den = den + lax.dot_general(p, jnp.ones((p.shape[0], LANES), BF16), contract_rows,
                                        preferred_element_type=F32)
        o_ref[0] = acc * (1.0 / den[:, :HEAD_DIM])


def _sample_attn(qt, cache_k, cache_v, k_new, v_new):
    bd = qt.shape[0]
    sq = DEC_T * N_HEADS
    far, near, new = (jnp.asarray(t) for t in _sample_mult_tables())
    tile = (N_HEADS, HEAD_DIM)
    chunk_spec = pl.BlockSpec((None, None, NEAR_ROWS) + tile, lambda b, c: (0, b, c, 0, 0))
    per_chunk = NEAR_ROWS // SUBLANES
    next_spec = pl.BlockSpec(
        (None, None, SUBLANES) + tile,
        lambda b, c: (0, b, jnp.minimum((c + 1) * per_chunk, WIN_MAX // SUBLANES - 1), 0, 0))
    new_spec = pl.BlockSpec((None, DEC_T) + tile, lambda b, c: (b, 0, 0, 0))
    const = lambda a: pl.BlockSpec(a.shape, lambda b, c: (0, 0))
    far_scratch = pltpu.VMEM((FAR_GROUPS, DEC_T) + tile, F32)
    return pl.pallas_call(
        _sattn_kernel,
        grid=(bd, N_CHUNKS),
        in_specs=[pl.BlockSpec((1, HEAD_DIM, sq), lambda b, c: (b, 0, 0)),
                  chunk_spec, next_spec, chunk_spec, next_spec, new_spec, new_spec,
                  const(far), const(near), const(new)],
        out_specs=[pl.BlockSpec((1, sq, HEAD_DIM), lambda b, c: (b, 0, 0)), chunk_spec, chunk_spec],
        out_shape=[jax.ShapeDtypeStruct((bd, sq, HEAD_DIM), F32),
                   jax.ShapeDtypeStruct(cache_k.shape, cache_k.dtype),
                   jax.ShapeDtypeStruct(cache_v.shape, cache_v.dtype)],
        scratch_shapes=[far_scratch, far_scratch],
        compiler_params=_params(("arbitrary", "arbitrary")),
        name="sample_attn",
    )(qt, cache_k, cache_k, cache_v, cache_v, k_new, v_new, far, near, new)


def _outproj_kernel(*refs, n_pat):
    a_ref = refs[0]
    o_refs = refs[1:1 + n_pat]
    pos = 1 + n_pat
    lse_refs = ()
    if n_pat > 1:
        lse_refs = refs[pos:pos + n_pat]
        pos += n_pat
    x_ref, g1_ref, sc_ref, sh_ref, gn_ref, w_ref, e_ref, x1_ref, h2_ref = refs[pos:]

    if n_pat > 1:
        lses = [r[0] for r in lse_refs]
        mx = functools.reduce(jnp.maximum, lses)
        es = [jnp.exp(l - mx) for l in lses]
        inv = 1.0 / functools.reduce(lambda a, c: a + c, es)
        e16 = e_ref[...]
        o = None
        for ep, o_ref in zip(es, o_refs):
            wp = ep * inv
            hi = wp.astype(BF16)
            lo = (wp - hi.astype(F32)).astype(BF16)
            wb = (jnp.dot(hi, e16, preferred_element_type=F32)
                  + jnp.dot(lo, e16, preferred_element_type=F32))
            term = wb * o_ref[0].astype(F32)
            o = term if o is None else o + term
        ob = o.astype(BF16)
    else:
        ob = o_refs[0][0].astype(BF16)

    mix = (jnp.dot(a_ref[0].astype(BF16), w_ref[0:D_CONV, :], preferred_element_type=F32)
           + jnp.dot(ob, w_ref[D_CONV:, :], preferred_element_type=F32))
    x1 = x_ref[0] + g1_ref[0] * mix
    x1_ref[0] = x1
    ms = jnp.mean(x1 * x1, axis=-1, keepdims=True)
    h2 = x1 * lax.rsqrt(ms + EPS) * gn_ref[...]
    h2_ref[0] = (h2 * (1.0 + sc_ref[0]) + sh_ref[0]).astype(h2_ref.dtype)


def _out_proj(a_out, outs, lses, x, mod, gn, w_out, e16, *, tm):
    bsz, seq, d = x.shape
    n_pat = len(outs)
    lm = mod.shape[1]
    mod_rows = tm if lm == seq else 1

    def mod_map(piece):
        if lm == seq:
            return lambda b, i: (b, i, piece)
        return lambda b, i: (b, 0, piece)

    half_spec = pl.BlockSpec((1, tm, D_CONV), lambda b, i: (b, i, 0))
    full_spec = pl.BlockSpec((1, tm, d), lambda b, i: (b, i, 0))
    in_specs = [half_spec] * (1 + n_pat)
    args = [a_out, *outs]
    if n_pat > 1:
        in_specs += [pl.BlockSpec((1, tm, N_HEADS), lambda b, i: (b, i, 0))] * n_pat
        args += list(lses)
    in_specs += [full_spec,
                 pl.BlockSpec((1, mod_rows, d), mod_map(2)),
                 pl.BlockSpec((1, mod_rows, d), mod_map(4)),
                 pl.BlockSpec((1, mod_rows, d), mod_map(3)),
                 pl.BlockSpec((1, d), lambda b, i: (0, 0)),
                 pl.BlockSpec(w_out.shape, lambda b, i: (0, 0)),
                 pl.BlockSpec(e16.shape, lambda b, i: (0, 0))]
    args += [x, mod, mod, mod, gn, w_out, e16]
    return pl.pallas_call(
        functools.partial(_outproj_kernel, n_pat=n_pat),
        grid=(bsz, seq // tm),
        in_specs=in_specs,
        out_specs=[full_spec, full_spec],
        out_shape=[jax.ShapeDtypeStruct((bsz, seq, d), F32), jax.ShapeDtypeStruct((bsz, seq, d), BF16)],
        compiler_params=_params(("parallel", "parallel")),
        name="out_proj",
    )(*args)


def _ffn_kernel(*refs, tm, rs, halo, has_prev, nf):
    (h2_ref, x1_ref, g2_ref, wg_ref, wv_ref, wd_ref, cwg_ref, cwv_ref, cbg_ref, cbv_ref) = refs[:10]
    pos = 10
    prev_refs = (None, None)
    if has_prev:
        prev_refs = refs[pos:pos + 2]
        pos += 2
    y_ref, tg_ref, tv_ref, acc_ref, ug_buf, uv_buf, carry_ref = refs[pos:]
    m = pl.program_id(1)
    f = pl.program_id(2)
    h2 = h2_ref[0]

    def conv_half(w_ref, cw_ref, cb_ref, prev_ref, ubuf, t_ref, slot):
        u = jnp.dot(h2, w_ref[...], preferred_element_type=F32)

        @pl.when(m == 0)
        def _():
            if has_prev:
                ubuf[0:halo, :] = prev_ref[0]
            else:
                ubuf[0:halo, :] = jnp.zeros((halo, u.shape[1]), F32)

        @pl.when(m > 0)
        def _():
            ubuf[0:halo, :] = carry_ref[f, slot]

        ubuf[halo:halo + tm, :] = u
        last = u[tm - halo:, :]
        carry_ref[f, slot] = last
        t_ref[0] = last
        return (cb_ref[...] + cw_ref[0:1, :] * ubuf[halo - 2 * rs:halo - 2 * rs + tm, :]
                + cw_ref[1:2, :] * ubuf[halo - rs:halo - rs + tm, :] + cw_ref[2:3, :] * u)

    gate = conv_half(wg_ref, cwg_ref, cbg_ref, prev_refs[0], ug_buf, tg_ref, 0)
    val = conv_half(wv_ref, cwv_ref, cbv_ref, prev_refs[1], uv_buf, tv_ref, 1)
    act = (gate * _sigmoid(gate) * val).astype(BF16)
    contrib = jnp.dot(act, wd_ref[...], preferred_element_type=F32)

    @pl.when(f == 0)
    def _():
        acc_ref[...] = contrib

    @pl.when(f > 0)
    def _():
        acc_ref[...] += contrib

    @pl.when(f == nf - 1)
    def _():
        y_ref[0] = x1_ref[0] + g2_ref[0] * acc_ref[...]


def _conv_ffn(h2, x1, mod, w_up, w_down, cw, cb, prev, *, tm, tf, rs):
    bsz, seq, d = x1.shape
    nf = D_FF // tf
    has_prev = prev is not None
    halo = 2 * rs if has_prev else SUBLANES
    lm = mod.shape[1]
    mod_rows = tm if lm == seq else 1
    g2_map = (lambda b, m, f: (b, m, 5)) if lm == seq else (lambda b, m, f: (b, 0, 5))
    row = lambda b, m, f: (b, m, 0)
    gate_col = lambda b, m, f: (0, f)
    val_col = lambda b, m, f: (0, nf + f)
    in_specs = [pl.BlockSpec((1, tm, d), row), pl.BlockSpec((1, tm, d), row),
                pl.BlockSpec((1, mod_rows, d), g2_map),
                pl.BlockSpec((d, tf), gate_col), pl.BlockSpec((d, tf), val_col),
                pl.BlockSpec((tf, d), lambda b, m, f: (f, 0)),
                pl.BlockSpec((FFN_CONV_WIDTH, tf), gate_col), pl.BlockSpec((FFN_CONV_WIDTH, tf), val_col),
                pl.BlockSpec((1, tf), gate_col), pl.BlockSpec((1, tf), val_col)]
    args = [h2, x1, mod, w_up, w_up, w_down, cw, cw, cb, cb]
    if has_prev:
        in_specs += [pl.BlockSpec((1, halo, tf), lambda b, m, f: (b, 0, f)),
                     pl.BlockSpec((1, halo, tf), lambda b, m, f: (b, 0, nf + f))]
        args += [prev, prev]
    nm = seq // tm
    tail_spec = pl.BlockSpec((1, halo, tf), lambda b, m, f: (b, m, f))
    tail_shape = jax.ShapeDtypeStruct((bsz, nm * halo, D_FF), F32)
    y, tail_g, tail_v = pl.pallas_call(
        functools.partial(_ffn_kernel, tm=tm, rs=rs, halo=halo, has_prev=has_prev, nf=nf),
        grid=(bsz, nm, nf),
        in_specs=in_specs,
        out_specs=[pl.BlockSpec((1, tm, d), row), tail_spec, tail_spec],
        out_shape=[jax.ShapeDtypeStruct((bsz, seq, d), F32), tail_shape, tail_shape],
        scratch_shapes=[pltpu.VMEM((tm, d), F32),
                        pltpu.VMEM((halo + tm, tf), F32), pltpu.VMEM((halo + tm, tf), F32),
                        pltpu.VMEM((nf, 2, halo, tf), F32)],
        compiler_params=_params(("arbitrary", "arbitrary", "arbitrary")),
        name="conv_ffn",
    )(*args)
    last = (nm - 1) * halo
    return y, jnp.concatenate([tail_g[:, last:], tail_v[:, last:]], axis=-1)


def _rope_tables(pos):
    half = HEAD_DIM // 2
    inv = ROPE_THETA ** (-jnp.arange(half, dtype=F32) / half)
    ang = pos.astype(F32)[:, None] * inv[None, :]
    cos = jnp.cos(ang)
    sin = jnp.sin(ang)
    cos_l = jnp.concatenate([cos, cos, cos, cos], axis=-1)
    sin_l = jnp.concatenate([-sin, sin, -sin, sin], axis=-1)
    return cos_l, sin_l


def _head_mean_matrix():
    h = np.arange(D_ATTN) // HEAD_DIM
    return jnp.asarray((h[:, None] == h[None, :]).astype(np.float32) / HEAD_DIM, dtype=BF16)


def _head_expand_matrix():
    h = np.arange(D_ATTN) // HEAD_DIM
    return jnp.asarray((np.arange(N_HEADS)[:, None] == h[None, :]).astype(np.float32), dtype=BF16)


def kernel(x_prompt, x_sample, cache_win_k, cache_win_v, state_conv_a, state_ffn_conv, c_prompt, c_sample,
           norm_mix_g, norm_ffn_g, w_ada, b_ada, w_in, conv_a_w, conv_a_b, ln_a_g, ln_a_b, q_norm_g, k_norm_g,
           w_out, w_up, ffn_conv_w, ffn_conv_b, w_down):
    bsz, seq, d = x_prompt.shape
    bd, dt, _ = x_sample.shape
    assert w_ada.shape[0] == 1 and dt == DEC_T and cache_win_k.shape[2] == WIN_MAX

    w_in_b = w_in[0].astype(BF16)
    w_out_b = w_out[0].astype(BF16)
    w_up_b = w_up[0].astype(BF16)
    w_down_b = w_down[0].astype(BF16)
    e_mean = _head_mean_matrix()
    e16 = _head_expand_matrix()
    qg = jnp.tile(q_norm_g[0], N_HEADS)[None, :]
    kg = jnp.tile(k_norm_g[0], N_HEADS)[None, :]

    c_all = jnp.concatenate([c_prompt, c_sample, jnp.zeros((16 - bsz - bd, d), F32)], axis=0)
    mod = _ada(c_all, w_ada[0], b_ada)
    mod_p = mod[:bsz][:, None, :]
    mod_s = jnp.tile(mod[bsz:bsz + bd], (dt, 1))[None]

    cos_p, sin_p = _rope_tables(jnp.arange(seq))
    g_p, q_p, k_p, v_p, g_tail, k_tail, v_tail = _in_proj(
        x_prompt, mod_p, mod_p, norm_mix_g, w_in_b, qg, kg, cos_p, sin_p, e_mean,
        tm=256, out_dtype=BF16, tails=True)
    a_p = _conv_mix(g_p, None, conv_a_w[0], conv_a_b, ln_a_g, ln_a_b, tc=256, rs=1, out_dtype=BF16)
    outs, lses = [], []
    for (window, dil), width in zip(DIL_PATTERNS, (128, 512, 1024)):
        o, lse = _band_attn(q_p, k_p, v_p, dil, width)
        outs.append(o)
        lses.append(lse)
    x1_p, h2_p = _out_proj(a_p, outs, lses, x_prompt, mod_p, norm_ffn_g, w_out_b, e16, tm=256)
    y_p, ftail_p = _conv_ffn(h2_p, x1_p, mod_p, w_up_b, w_down_b, ffn_conv_w[0], ffn_conv_b, None,
                             tm=512, tf=512, rs=1)

    rows = dt * bd
    to_tm = lambda a: a.transpose(1, 0, 2).reshape(1, a.shape[0] * a.shape[1], a.shape[2])
    xs = to_tm(x_sample)
    cos_s, sin_s = _rope_tables(PAST_LEN + jnp.arange(rows) // bd)
    g_s, q_s, k_s, v_s = _in_proj(xs, mod_s, mod_s, norm_mix_g, w_in_b, qg, kg, cos_s, sin_s, e_mean,
                                  tm=rows, out_dtype=F32, tails=False)
    a_s = _conv_mix(g_s, to_tm(state_conv_a[0]), conv_a_w[0], conv_a_b, ln_a_g, ln_a_b,
                    tc=rows, rs=bd, out_dtype=BF16)
    heads = lambda a: a.reshape(dt, bd, N_HEADS, HEAD_DIM)
    qt = heads(q_s).transpose(1, 3, 0, 2).reshape(bd, HEAD_DIM, dt * N_HEADS).astype(BF16)
    k_new = heads(k_s).transpose(1, 0, 2, 3)
    v_new = heads(v_s).transpose(1, 0, 2, 3)
    o_s, win_k_s, win_v_s = _sample_attn(qt, cache_win_k, cache_win_v, k_new, v_new)
    o_s = o_s.reshape(bd, dt, N_HEADS * HEAD_DIM).transpose(1, 0, 2).reshape(1, rows, D_ATTN)
    x1_s, h2_s = _out_proj(a_s, [o_s], None, xs, mod_s, norm_ffn_g, w_out_b, e16, tm=rows)
    y_s, ftail_s = _conv_ffn(h2_s, x1_s, mod_s, w_up_b, w_down_b, ffn_conv_w[0], ffn_conv_b,
                             to_tm(state_ffn_conv[0]), tm=rows, tf=512, rs=bd)

    from_tm = lambda a, t: a.reshape(t, bd, a.shape[-1]).transpose(1, 0, 2)
    keep = min(WIN_MAX, seq)
    out_heads = lambda a: a.reshape(1, bsz, keep, N_HEADS, HEAD_DIM)
    conv_a_s = jnp.concatenate([state_conv_a[0], from_tm(g_s[0], dt)], axis=1)[:, dt:]
    return (y_p, from_tm(y_s[0], dt),
            out_heads(k_tail), out_heads(v_tail),
            g_tail[:, 32 - (CONV_A_WIDTH - 1):][None],
            ftail_p[:, SUBLANES - (FFN_CONV_WIDTH - 1):][None],
            win_k_s, win_v_s,
            conv_a_s[None],
            from_tm(ftail_s[0], FFN_CONV_WIDTH - 1)[None])
```

```python
import functools

import numpy as np
import jax
import jax.numpy as jnp
from jax import lax
from jax.experimental import pallas as pl
from jax.experimental.pallas import tpu as pltpu

F32 = jnp.float32
BF16 = jnp.bfloat16

D_MODEL = 2048
HEAD_DIM = 64
D_CONV = D_MODEL // 2
N_HEADS = (D_MODEL // 2) // HEAD_DIM
D_ATTN = N_HEADS * HEAD_DIM
CONV_A_WIDTH = 31
DIL_PATTERNS = ((128, 1), (512, 4), (2048, 16))
BAND = 128
WIN_MAX = 2048
ROPE_THETA = 10000.0
D_FF = 5632
FFN_CONV_WIDTH = 3
EPS = 1e-6
NEG_INF = -1e30
PAST_LEN = 16384

LANES = 128
SUBLANES = 8
VMEM_LIMIT = 56 * 1024 * 1024
IN_PROJ_VMEM_LIMIT = 60 * 1024 * 1024


def _sigmoid(x):
    return 1.0 / (1.0 + jnp.exp(-x))


def _params(sem, vmem=VMEM_LIMIT):
    return pltpu.CompilerParams(dimension_semantics=sem, vmem_limit_bytes=vmem)


def _ada_kernel(c_ref, w_ref, b_ref, o_ref):
    c = c_ref[...]
    s = (c * _sigmoid(c)).astype(BF16)
    o_ref[...] = jnp.dot(s, w_ref[...].astype(BF16), preferred_element_type=F32) + b_ref[...]


def _ada(c_all, w_ada, b_ada):
    rows, d = c_all.shape
    n = w_ada.shape[1]
    tn = 1024
    return pl.pallas_call(
        _ada_kernel,
        grid=(n // tn,),
        in_specs=[pl.BlockSpec((rows, d), lambda j: (0, 0)),
                  pl.BlockSpec((d, tn), lambda j: (0, j)),
                  pl.BlockSpec((1, tn), lambda j: (0, j))],
        out_specs=pl.BlockSpec((rows, tn), lambda j: (0, j)),
        out_shape=jax.ShapeDtypeStruct((rows, n), F32),
        compiler_params=_params(("parallel",)),
        name="ada_mod",
    )(c_all, w_ada, b_ada)


def _head_norm_rope(x, gain, cos, sin, e):
    ms = jnp.dot((x * x).astype(BF16), e, preferred_element_type=F32)
    xn = x * lax.rsqrt(ms + EPS) * gain
    lane = lax.broadcasted_iota(jnp.int32, (1, LANES), 1)
    first_half = jnp.bitwise_and(lane, HEAD_DIM - 1) < (HEAD_DIM // 2)
    out = []
    for c in range(D_ATTN // LANES):
        xc = xn[:, c * LANES:(c + 1) * LANES]
        partner = jnp.where(first_half, pltpu.roll(xc, LANES - HEAD_DIM // 2, 1),
                            pltpu.roll(xc, HEAD_DIM // 2, 1))
        out.append(xc * cos + partner * sin)
    return out


def _inproj_kernel(x_ref, sc_ref, sh_ref, gmix_ref, w_ref, qg_ref, kg_ref, cos_ref, sin_ref, e_ref,
                   g_ref, q_ref, k_ref, v_ref, *extra_refs, nt, prompt):
    x = x_ref[0]
    ms = jnp.mean(x * x, axis=-1, keepdims=True)
    h = x * lax.rsqrt(ms + EPS) * gmix_ref[...]
    hb = (h * (1.0 + sc_ref[0]) + sh_ref[0]).astype(BF16)

    def proj(j):
        return jnp.dot(hb, w_ref[:, j * D_CONV:(j + 1) * D_CONV], preferred_element_type=F32)

    g = proj(0) * _sigmoid(proj(1))
    g_ref[0] = g.astype(g_ref.dtype)
    cos = cos_ref[...]
    sin = sin_ref[...]
    e = e_ref[...]
    q = jnp.concatenate(_head_norm_rope(proj(2), qg_ref[...], cos, sin, e), axis=1) * (HEAD_DIM ** -0.5)
    k = jnp.concatenate(_head_norm_rope(proj(3), kg_ref[...], cos, sin, e), axis=1)
    v = proj(4)
    qkv = [a.astype(q_ref.dtype) for a in (q, k, v)]
    for ref, a in zip((q_ref, k_ref, v_ref), qkv):
        ref[0] = a

    if prompt:
        view_refs = extra_refs[:6]
        gt_ref, kt_ref, vt_ref = extra_refs[6:]
        tm = x.shape[0]
        for n, (_, dil) in enumerate(DIL_PATTERNS[1:]):
            for ref, a in zip(view_refs[3 * n:3 * n + 3], qkv):
                ref[0] = a.reshape(tm // dil, dil * D_ATTN)
        i = pl.program_id(1)
        rows = gt_ref.shape[1]

        @pl.when(i == nt - 1)
        def _():
            gt_ref[0] = g[g.shape[0] - rows:, :]

        @pl.when(i >= nt // 2)
        def _():
            kt_ref[0] = k.reshape(tm, N_HEADS, HEAD_DIM)
            vt_ref[0] = v.reshape(tm, N_HEADS, HEAD_DIM)


def _in_proj(x, sc, sh, gmix, w_in, qg, kg, cos, sin, e, *, tm, out_dtype, tails):
    bsz, seq, d = x.shape
    nt = seq // tm
    lm = sc.shape[1]
    mod_rows = tm if lm == seq else 1

    def mod_map(piece):
        if lm == seq:
            return lambda b, i: (b, i, piece)
        return lambda b, i: (b, 0, piece)

    row_spec = pl.BlockSpec((1, tm, D_CONV), lambda b, i: (b, i, 0))
    in_specs = [
        pl.BlockSpec((1, tm, d), lambda b, i: (b, i, 0)),
        pl.BlockSpec((1, mod_rows, d), mod_map(1)),
        pl.BlockSpec((1, mod_rows, d), mod_map(0)),
        pl.BlockSpec((1, d), lambda b, i: (0, 0)),
        pl.BlockSpec(w_in.shape, lambda b, i: (0, 0)),
        pl.BlockSpec((1, D_ATTN), lambda b, i: (0, 0)),
        pl.BlockSpec((1, D_ATTN), lambda b, i: (0, 0)),
        pl.BlockSpec((tm, LANES), lambda b, i: (i, 0)),
        pl.BlockSpec((tm, LANES), lambda b, i: (i, 0)),
        pl.BlockSpec(e.shape, lambda b, i: (0, 0)),
    ]
    out_specs = [row_spec] * 4
    out_shape = [jax.ShapeDtypeStruct((bsz, seq, D_CONV), out_dtype)] * 4
    if tails:
        half = nt // 2
        tail_map = lambda b, i: (b, jnp.maximum(i - half, 0), 0, 0)
        for _, dil in DIL_PATTERNS[1:]:
            out_specs = out_specs + [pl.BlockSpec((1, tm // dil, dil * D_ATTN), lambda b, i: (b, i, 0))] * 3
            out_shape = out_shape + [jax.ShapeDtypeStruct((bsz, seq // dil, dil * D_ATTN), out_dtype)] * 3
        out_specs = out_specs + [pl.BlockSpec((1, 32, D_CONV), lambda b, i: (b, 0, 0)),
                                 pl.BlockSpec((1, tm, N_HEADS, HEAD_DIM), tail_map),
                                 pl.BlockSpec((1, tm, N_HEADS, HEAD_DIM), tail_map)]
        out_shape = out_shape + [jax.ShapeDtypeStruct((bsz, 32, D_CONV), F32),
                                 jax.ShapeDtypeStruct((bsz, seq // 2, N_HEADS, HEAD_DIM), F32),
                                 jax.ShapeDtypeStruct((bsz, seq // 2, N_HEADS, HEAD_DIM), F32)]
    return pl.pallas_call(
        functools.partial(_inproj_kernel, nt=nt, prompt=tails),
        grid=(bsz, nt),
        in_specs=in_specs,
        out_specs=out_specs,
        out_shape=out_shape,
        compiler_params=_params(("arbitrary", "arbitrary"), IN_PROJ_VMEM_LIMIT),
        name="in_proj",
    )(x, sc, sh, gmix, w_in, qg, kg, cos, sin, e)


def _conv_kernel(*refs, tc, rs, halo, has_prev, rc):
    if has_prev:
        g_ref, prev_ref, w_ref, b_ref, lg_ref, lb_ref, o_ref, buf = refs
    else:
        g_ref, w_ref, b_ref, lg_ref, lb_ref, o_ref, buf = refs
    i = pl.program_id(1)

    @pl.when(i == 0)
    def _():
        if has_prev:
            buf[0:halo, :] = prev_ref[0]
        else:
            buf[0:halo, :] = jnp.zeros((halo, D_CONV), F32)

    @pl.when(i > 0)
    def _():
        buf[0:halo, :] = buf[tc:tc + halo, :]

    buf[halo:halo + tc, :] = g_ref[0].astype(F32)
    base = halo - (CONV_A_WIDTH - 1) * rs
    bias = b_ref[...]
    lg = lg_ref[...]
    lb = lb_ref[...]
    for c in range(tc // rc):
        r0 = c * rc
        acc = jnp.broadcast_to(bias, (rc, D_CONV))
        if rs % SUBLANES == 0:
            for k in range(CONV_A_WIDTH):
                off = r0 + base + k * rs
                acc = acc + w_ref[k:k + 1, :] * buf[off:off + rc, :]
        else:
            n = rc + halo
            ext = buf[r0:r0 + n, :]
            for s in range(SUBLANES):
                rolled = ext if s == 0 else pltpu.roll(ext, n - s, 0)
                for k in range(CONV_A_WIDTH):
                    off = base + k * rs
                    if off % SUBLANES == s:
                        acc = acc + w_ref[k:k + 1, :] * rolled[off - s:off - s + rc, :]
        mu = jnp.mean(acc, axis=-1, keepdims=True)
        xc = acc - mu
        var = jnp.mean(xc * xc, axis=-1, keepdims=True)
        y = xc * lax.rsqrt(var + EPS) * lg + lb
        o_ref[0, r0:r0 + rc, :] = (y * _sigmoid(y)).astype(o_ref.dtype)


def _conv_mix(g, prev, w, b, lg, lb, *, tc, rs, out_dtype):
    bsz, seq, ch = g.shape
    has_prev = prev is not None
    halo = prev.shape[1] if has_prev else 32
    rc = min(tc, 32)
    in_specs = [pl.BlockSpec((1, tc, ch), lambda bb, i: (bb, i, 0))]
    args = [g]
    if has_prev:
        in_specs.append(pl.BlockSpec((1, halo, ch), lambda bb, i: (bb, 0, 0)))
        args.append(prev)
    in_specs += [pl.BlockSpec(w.shape, lambda bb, i: (0, 0))] + [pl.BlockSpec((1, ch), lambda bb, i: (0, 0))] * 3
    args += [w, b, lg, lb]
    return pl.pallas_call(
        functools.partial(_conv_kernel, tc=tc, rs=rs, halo=halo, has_prev=has_prev, rc=rc),
        grid=(bsz, seq // tc),
        in_specs=in_specs,
        out_specs=pl.BlockSpec((1, tc, ch), lambda bb, i: (bb, i, 0)),
        out_shape=jax.ShapeDtypeStruct((bsz, seq, ch), out_dtype),
        scratch_shapes=[pltpu.VMEM((halo + tc, ch), F32)],
        compiler_params=_params(("arbitrary", "arbitrary")),
        name="conv_mix",
    )(*args)


def _band_attn_kernel(q_ref, k_ref, v_ref, o_ref, lse_ref, *, lr, hb, ub):
    nblk = lr // BAND
    qi = lax.broadcasted_iota(jnp.int32, (BAND, 2 * BAND), 0)
    kj = lax.broadcasted_iota(jnp.int32, (BAND, 2 * BAND), 1)
    bias_band = jnp.where(kj >= qi, jnp.where(kj <= qi + BAND, 0.0, NEG_INF), NEG_INF).astype(F32)
    qi1 = lax.broadcasted_iota(jnp.int32, (BAND, BAND), 0)
    kj1 = lax.broadcasted_iota(jnp.int32, (BAND, BAND), 1)
    bias_first = jnp.where(kj1 <= qi1, 0.0, NEG_INF).astype(F32)

    def block(q0, k0, nk, bias):
        for hh in range(hb):
            lo = hh * HEAD_DIM
            qb = q_ref[0, pl.ds(q0, BAND), lo:lo + HEAD_DIM]
            kb = k_ref[0, pl.ds(k0, nk), lo:lo + HEAD_DIM]
            vb = v_ref[0, pl.ds(k0, nk), lo:lo + HEAD_DIM]
            s = lax.dot_general(qb, kb, (((1,), (1,)), ((), ())), preferred_element_type=F32) + bias
            m = jnp.max(s, axis=-1, keepdims=True)
            p = jnp.exp(s - m)
            l = jnp.sum(p, axis=-1, keepdims=True)
            o = jnp.dot(p.astype(BF16), vb, preferred_element_type=F32) * (1.0 / l)
            o_ref[0, pl.ds(q0, BAND), lo:lo + HEAD_DIM] = o.astype(o_ref.dtype)
            lse_ref[0, 0, 0, pl.ds(q0, BAND), hh:hh + 1] = m + jnp.log(l)

    block(0, 0, BAND, bias_first)

    def banded(i):
        block(pl.multiple_of(i * BAND, BAND), pl.multiple_of((i - 1) * BAND, BAND), 2 * BAND, bias_band)

    groups = (nblk - 1) // ub

    def body(gi, carry):
        for j in range(ub):
            banded(1 + gi * ub + j)
        return carry

    lax.fori_loop(0, groups, body, 0)
    for i in range(1 + groups * ub, nblk):
        banded(i)


def _band_attn(q, k, v, dil, width, ub):
    bsz, lr, cols = q.shape
    da = cols // dil
    seq = lr * dil
    hb = width // HEAD_DIM
    ncol = da // width
    spec = pl.BlockSpec((1, lr, width), lambda b, c: (b, 0, c))
    o, lse = pl.pallas_call(
        functools.partial(_band_attn_kernel, lr=lr, hb=hb, ub=ub),
        grid=(bsz, dil * ncol),
        in_specs=[spec, spec, spec],
        out_specs=[spec,
                   pl.BlockSpec((1, 1, 1, lr, hb), lambda b, c: (b, c // ncol, c % ncol, 0, 0))],
        out_shape=[jax.ShapeDtypeStruct((bsz, lr, cols), BF16),
                   jax.ShapeDtypeStruct((bsz, dil, ncol, lr, hb), F32)],
        compiler_params=_params(("parallel", "parallel")),
        name=f"band_attn_d{dil}",
    )(q, k, v)
    lse = lse.transpose(0, 3, 1, 2, 4).reshape(bsz, seq, N_HEADS)
    return o, lse


FAR_GROUPS = (WIN_MAX - 512) // 16
NEAR_ROWS = 512
DEC_T = 4


def _sample_mult_tables():
    col = np.arange(DEC_T * N_HEADS)
    t, hq = col // N_HEADS, col % N_HEADS

    def table(r, h):
        d = WIN_MAX + t[None, :] - r[:, None]
        m = np.zeros(d.shape, np.float32)
        for window, dil in DIL_PATTERNS:
            m += ((d >= 0) & (d <= window) & (d % dil == 0)).astype(np.float32)
        return m * (h[:, None] == hq[None, :])

    idx = np.arange(FAR_GROUPS * DEC_T * N_HEADS)
    far = table(16 * (idx // (DEC_T * N_HEADS)) + (idx // N_HEADS) % DEC_T, idx % N_HEADS)
    idx = np.arange(NEAR_ROWS * N_HEADS)
    near = table(WIN_MAX - NEAR_ROWS + idx // N_HEADS, idx % N_HEADS)
    idx = np.arange(DEC_T * N_HEADS)
    new = table(WIN_MAX + idx // N_HEADS, idx % N_HEADS)
    return far, near, new


N_CHUNKS = WIN_MAX // NEAR_ROWS
GROUPS_PER_CHUNK = NEAR_ROWS // 16


def _sattn_kernel(qt_ref, ck_ref, ckn_ref, cv_ref, cvn_ref, kn_ref, vn_ref, mf_ref, mn_ref, mw_ref,
                  o_ref, ok_ref, ov_ref, fk_ref, fv_ref):
    c = pl.program_id(1)
    keep = NEAR_ROWS - DEC_T

    ok_ref[0:keep] = ck_ref[DEC_T:NEAR_ROWS]
    ov_ref[0:keep] = cv_ref[DEC_T:NEAR_ROWS]

    @pl.when(c < N_CHUNKS - 1)
    def _():
        ok_ref[keep:NEAR_ROWS] = ckn_ref[0:DEC_T]
        ov_ref[keep:NEAR_ROWS] = cvn_ref[0:DEC_T]
        g0 = c * GROUPS_PER_CHUNK
        for src, dst in ((ck_ref, fk_ref), (cv_ref, fv_ref)):
            grouped = src[...].reshape(GROUPS_PER_CHUNK, 16, N_HEADS, HEAD_DIM)
            dst[pl.ds(g0, GROUPS_PER_CHUNK)] = grouped[:, 0:DEC_T]

    @pl.when(c == N_CHUNKS - 1)
    def _():
        ok_ref[keep:NEAR_ROWS] = kn_ref[...]
        ov_ref[keep:NEAR_ROWS] = vn_ref[...]
        qt = qt_ref[0]
        flat = lambda r: r[...].reshape(-1, HEAD_DIM)
        keys = [flat(fk_ref), flat(ck_ref), flat(kn_ref)]
        vals = [flat(fv_ref), flat(cv_ref), flat(vn_ref)]
        mults = [mf_ref[...], mn_ref[...], mw_ref[...]]
        scores = []
        m = None
        for kk, mu in zip(keys, mults):
            s = jnp.dot(kk.astype(BF16), qt, preferred_element_type=F32)
            s = jnp.where(mu > 0.0, s, NEG_INF)
            scores.append(s)
            sm = jnp.max(s, axis=0, keepdims=True)
            m = sm if m is None else jnp.maximum(m, sm)
        acc = jnp.zeros((DEC_T * N_HEADS, HEAD_DIM), F32)
        den = jnp.zeros((DEC_T * N_HEADS, LANES), F32)
        contract_rows = (((0,), (0,)), ((), ()))
        for s, mu, vv in zip(scores, mults, vals):
            p = (mu * jnp.exp(s - m)).astype(BF16)
            acc = acc + lax.dot_general(p, vv.astype(BF16), contract_rows, preferred_element_type=F32)
            den = den + lax.dot_general(p, jnp.ones((p.shape[0], LANES), BF16), contract_rows,
                                        preferred_element_type=F32)
        o_ref[0] = acc * (1.0 / den[:, :HEAD_DIM])


def _sample_attn(qt, cache_k, cache_v, k_new, v_new):
    bd = qt.shape[0]
    sq = DEC_T * N_HEADS
    far, near, new = (jnp.asarray(t) for t in _sample_mult_tables())
    tile = (N_HEADS, HEAD_DIM)
    chunk_spec = pl.BlockSpec((None, None, NEAR_ROWS) + tile, lambda b, c: (0, b, c, 0, 0))
    per_chunk = NEAR_ROWS // SUBLANES
    next_spec = pl.BlockSpec(
        (None, None, SUBLANES) + tile,
        lambda b, c: (0, b, jnp.minimum((c + 1) * per_chunk, WIN_MAX // SUBLANES - 1), 0, 0))
    new_spec = pl.BlockSpec((None, DEC_T) + tile, lambda b, c: (b, 0, 0, 0))
    const = lambda a: pl.BlockSpec(a.shape, lambda b, c: (0, 0))
    far_scratch = pltpu.VMEM((FAR_GROUPS, DEC_T) + tile, F32)
    return pl.pallas_call(
        _sattn_kernel,
        grid=(bd, N_CHUNKS),
        in_specs=[pl.BlockSpec((1, HEAD_DIM, sq), lambda b, c: (b, 0, 0)),
                  chunk_spec, next_spec, chunk_spec, next_spec, new_spec, new_spec,
                  const(far), const(near), const(new)],
        out_specs=[pl.BlockSpec((1, sq, HEAD_DIM), lambda b, c: (b, 0, 0)), chunk_spec, chunk_spec],
        out_shape=[jax.ShapeDtypeStruct((bd, sq, HEAD_DIM), F32),
                   jax.ShapeDtypeStruct(cache_k.shape, cache_k.dtype),
                   jax.ShapeDtypeStruct(cache_v.shape, cache_v.dtype)],
        scratch_shapes=[far_scratch, far_scratch],
        compiler_params=_params(("arbitrary", "arbitrary")),
        name="sample_attn",
    )(qt, cache_k, cache_k, cache_v, cache_v, k_new, v_new, far, near, new)


def _outproj_kernel(*refs, n_pat):
    a_ref = refs[0]
    o_refs = refs[1:1 + n_pat]
    pos = 1 + n_pat
    lse_refs = ()
    if n_pat > 1:
        lse_refs = refs[pos:pos + n_pat]
        pos += n_pat
    x_ref, g1_ref, sc_ref, sh_ref, gn_ref, w_ref, e_ref, x1_ref, h2_ref = refs[pos:]

    if n_pat > 1:
        lses = [r[0] for r in lse_refs]
        mx = functools.reduce(jnp.maximum, lses)
        es = [jnp.exp(l - mx) for l in lses]
        inv = 1.0 / functools.reduce(lambda a, c: a + c, es)
        e16 = e_ref[...]
        o = None
        for ep, o_ref in zip(es, o_refs):
            wp = ep * inv
            hi = wp.astype(BF16)
            lo = (wp - hi.astype(F32)).astype(BF16)
            wb = (jnp.dot(hi, e16, preferred_element_type=F32)
                  + jnp.dot(lo, e16, preferred_element_type=F32))
            term = wb * o_ref[0].reshape(wb.shape).astype(F32)
            o = term if o is None else o + term
        ob = o.astype(BF16)
    else:
        ob = o_refs[0][0].astype(BF16)

    mix = (jnp.dot(a_ref[0].astype(BF16), w_ref[0:D_CONV, :], preferred_element_type=F32)
           + jnp.dot(ob, w_ref[D_CONV:, :], preferred_element_type=F32))
    x1 = x_ref[0] + g1_ref[0] * mix
    x1_ref[0] = x1
    ms = jnp.mean(x1 * x1, axis=-1, keepdims=True)
    h2 = x1 * lax.rsqrt(ms + EPS) * gn_ref[...]
    h2_ref[0] = (h2 * (1.0 + sc_ref[0]) + sh_ref[0]).astype(h2_ref.dtype)


def _out_proj(a_out, outs, lses, x, mod, gn, w_out, e16, *, tm):
    bsz, seq, d = x.shape
    n_pat = len(outs)
    lm = mod.shape[1]
    mod_rows = tm if lm == seq else 1

    def mod_map(piece):
        if lm == seq:
            return lambda b, i: (b, i, piece)
        return lambda b, i: (b, 0, piece)

    half_spec = pl.BlockSpec((1, tm, D_CONV), lambda b, i: (b, i, 0))
    full_spec = pl.BlockSpec((1, tm, d), lambda b, i: (b, i, 0))
    in_specs = [half_spec]
    for o in outs:
        dil = seq // o.shape[1]
        in_specs.append(pl.BlockSpec((1, tm // dil, dil * D_ATTN), lambda b, i: (b, i, 0)))
    args = [a_out, *outs]
    if n_pat > 1:
        in_specs += [pl.BlockSpec((1, tm, N_HEADS), lambda b, i: (b, i, 0))] * n_pat
        args += list(lses)
    in_specs += [full_spec,
                 pl.BlockSpec((1, mod_rows, d), mod_map(2)),
                 pl.BlockSpec((1, mod_rows, d), mod_map(4)),
                 pl.BlockSpec((1, mod_rows, d), mod_map(3)),
                 pl.BlockSpec((1, d), lambda b, i: (0, 0)),
                 pl.BlockSpec(w_out.shape, lambda b, i: (0, 0)),
                 pl.BlockSpec(e16.shape, lambda b, i: (0, 0))]
    args += [x, mod, mod, mod, gn, w_out, e16]
    return pl.pallas_call(
        functools.partial(_outproj_kernel, n_pat=n_pat),
        grid=(bsz, seq // tm),
        in_specs=in_specs,
        out_specs=[full_spec, full_spec],
        out_shape=[jax.ShapeDtypeStruct((bsz, seq, d), F32), jax.ShapeDtypeStruct((bsz, seq, d), BF16)],
        compiler_params=_params(("parallel", "parallel")),
        name="out_proj",
    )(*args)


def _ffn_kernel(*refs, tm, rs, halo, has_prev, nf):
    (h2_ref, x1_ref, g2_ref, wg_ref, wv_ref, wd_ref, cwg_ref, cwv_ref, cbg_ref, cbv_ref) = refs[:10]
    pos = 10
    prev_refs = (None, None)
    if has_prev:
        prev_refs = refs[pos:pos + 2]
        pos += 2
    y_ref, tg_ref, tv_ref, acc_ref, ug_buf, uv_buf, carry_ref = refs[pos:]
    m = pl.program_id(1)
    f = pl.program_id(2)
    h2 = h2_ref[0]

    def conv_half(w_ref, cw_ref, cb_ref, prev_ref, ubuf, t_ref, slot):
        u = jnp.dot(h2, w_ref[...], preferred_element_type=F32)

        @pl.when(m == 0)
        def _():
            if has_prev:
                ubuf[0:halo, :] = prev_ref[0]
            else:
                ubuf[0:halo, :] = jnp.zeros((halo, u.shape[1]), F32)

        @pl.when(m > 0)
        def _():
            ubuf[0:halo, :] = carry_ref[f, slot]

        ubuf[halo:halo + tm, :] = u
        last = u[tm - halo:, :]
        carry_ref[f, slot] = last
        t_ref[0] = last
        return (cb_ref[...] + cw_ref[0:1, :] * ubuf[halo - 2 * rs:halo - 2 * rs + tm, :]
                + cw_ref[1:2, :] * ubuf[halo - rs:halo - rs + tm, :] + cw_ref[2:3, :] * u)

    gate = conv_half(wg_ref, cwg_ref, cbg_ref, prev_refs[0], ug_buf, tg_ref, 0)
    val = conv_half(wv_ref, cwv_ref, cbv_ref, prev_refs[1], uv_buf, tv_ref, 1)
    act = (gate * _sigmoid(gate) * val).astype(BF16)
    contrib = jnp.dot(act, wd_ref[...], preferred_element_type=F32)

    @pl.when(f == 0)
    def _():
        acc_ref[...] = contrib

    @pl.when(f > 0)
    def _():
        acc_ref[...] += contrib

    @pl.when(f == nf - 1)
    def _():
        y_ref[0] = x1_ref[0] + g2_ref[0] * acc_ref[...]


def _conv_ffn(h2, x1, mod, w_up, w_down, cw, cb, prev, *, tm, tf, rs):
    bsz, seq, d = x1.shape
    nf = D_FF // tf
    has_prev = prev is not None
    halo = 2 * rs if has_prev else SUBLANES
    lm = mod.shape[1]
    mod_rows = tm if lm == seq else 1
    g2_map = (lambda b, m, f: (b, m, 5)) if lm == seq else (lambda b, m, f: (b, 0, 5))
    row = lambda b, m, f: (b, m, 0)
    gate_col = lambda b, m, f: (0, f)
    val_col = lambda b, m, f: (0, nf + f)
    in_specs = [pl.BlockSpec((1, tm, d), row), pl.BlockSpec((1, tm, d), row),
                pl.BlockSpec((1, mod_rows, d), g2_map),
                pl.BlockSpec((d, tf), gate_col), pl.BlockSpec((d, tf), val_col),
                pl.BlockSpec((tf, d), lambda b, m, f: (f, 0)),
                pl.BlockSpec((FFN_CONV_WIDTH, tf), gate_col), pl.BlockSpec((FFN_CONV_WIDTH, tf), val_col),
                pl.BlockSpec((1, tf), gate_col), pl.BlockSpec((1, tf), val_col)]
    args = [h2, x1, mod, w_up, w_up, w_down, cw, cw, cb, cb]
    if has_prev:
        in_specs += [pl.BlockSpec((1, halo, tf), lambda b, m, f: (b, 0, f)),
                     pl.BlockSpec((1, halo, tf), lambda b, m, f: (b, 0, nf + f))]
        args += [prev, prev]
    nm = seq // tm
    tail_spec = pl.BlockSpec((1, halo, tf), lambda b, m, f: (b, m, f))
    tail_shape = jax.ShapeDtypeStruct((bsz, nm * halo, D_FF), F32)
    y, tail_g, tail_v = pl.pallas_call(
        functools.partial(_ffn_kernel, tm=tm, rs=rs, halo=halo, has_prev=has_prev, nf=nf),
        grid=(bsz, nm, nf),
        in_specs=in_specs,
        out_specs=[pl.BlockSpec((1, tm, d), row), tail_spec, tail_spec],
        out_shape=[jax.ShapeDtypeStruct((bsz, seq, d), F32), tail_shape, tail_shape],
        scratch_shapes=[pltpu.VMEM((tm, d), F32),
                        pltpu.VMEM((halo + tm, tf), F32), pltpu.VMEM((halo + tm, tf), F32),
                        pltpu.VMEM((nf, 2, halo, tf), F32)],
        compiler_params=_params(("arbitrary", "arbitrary", "arbitrary")),
        name="conv_ffn",
    )(*args)
    last = (nm - 1) * halo
    return y, jnp.concatenate([tail_g[:, last:], tail_v[:, last:]], axis=-1)


def _rope_tables(pos):
    half = HEAD_DIM // 2
    inv = ROPE_THETA ** (-jnp.arange(half, dtype=F32) / half)
    ang = pos.astype(F32)[:, None] * inv[None, :]
    cos = jnp.cos(ang)
    sin = jnp.sin(ang)
    cos_l = jnp.concatenate([cos, cos, cos, cos], axis=-1)
    sin_l = jnp.concatenate([-sin, sin, -sin, sin], axis=-1)
    return cos_l, sin_l


def _head_mean_matrix():
    h = np.arange(D_ATTN) // HEAD_DIM
    return jnp.asarray((h[:, None] == h[None, :]).astype(np.float32) / HEAD_DIM, dtype=BF16)


def _head_expand_matrix():
    h = np.arange(D_ATTN) // HEAD_DIM
    return jnp.asarray((np.arange(N_HEADS)[:, None] == h[None, :]).astype(np.float32), dtype=BF16)


def kernel(x_prompt, x_sample, cache_win_k, cache_win_v, state_conv_a, state_ffn_conv, c_prompt, c_sample,
           norm_mix_g, norm_ffn_g, w_ada, b_ada, w_in, conv_a_w, conv_a_b, ln_a_g, ln_a_b, q_norm_g, k_norm_g,
           w_out, w_up, ffn_conv_w, ffn_conv_b, w_down):
    bsz, seq, d = x_prompt.shape
    bd, dt, _ = x_sample.shape
    assert w_ada.shape[0] == 1 and dt == DEC_T and cache_win_k.shape[2] == WIN_MAX and seq == 2 * WIN_MAX

    w_in_b = w_in[0].astype(BF16)
    w_out_b = w_out[0].astype(BF16)
    w_up_b = w_up[0].astype(BF16)
    w_down_b = w_down[0].astype(BF16)
    e_mean = _head_mean_matrix()
    e16 = _head_expand_matrix()
    qg = jnp.tile(q_norm_g[0], N_HEADS)[None, :]
    kg = jnp.tile(k_norm_g[0], N_HEADS)[None, :]

    c_all = jnp.concatenate([c_prompt, c_sample, jnp.zeros((16 - bsz - bd, d), F32)], axis=0)
    mod = _ada(c_all, w_ada[0], b_ada)
    mod_p = mod[:bsz][:, None, :]
    mod_s = jnp.tile(mod[bsz:bsz + bd], (dt, 1))[None]

    cos_p, sin_p = _rope_tables(jnp.arange(seq))
    g_p, *qkv_views, g_tail, k_tail, v_tail = _in_proj(
        x_prompt, mod_p, mod_p, norm_mix_g, w_in_b, qg, kg, cos_p, sin_p, e_mean,
        tm=256, out_dtype=BF16, tails=True)
    a_p = _conv_mix(g_p, None, conv_a_w[0], conv_a_b, ln_a_g, ln_a_b, tc=256, rs=1, out_dtype=BF16)
    outs, lses = [], []
    for n, ((window, dil), width, ub) in enumerate(zip(DIL_PATTERNS, (256, 512, 1024), (2, 1, 1))):
        o, lse = _band_attn(*qkv_views[3 * n:3 * n + 3], dil, width, ub)
        outs.append(o)
        lses.append(lse)
    x1_p, h2_p = _out_proj(a_p, outs, lses, x_prompt, mod_p, norm_ffn_g, w_out_b, e16, tm=256)
    y_p, ftail_p = _conv_ffn(h2_p, x1_p, mod_p, w_up_b, w_down_b, ffn_conv_w[0], ffn_conv_b, None,
                             tm=512, tf=512, rs=1)

    rows = dt * bd
    to_tm = lambda a: a.transpose(1, 0, 2).reshape(1, a.shape[0] * a.shape[1], a.shape[2])
    xs = to_tm(x_sample)
    cos_s, sin_s = _rope_tables(PAST_LEN + jnp.arange(rows) // bd)
    g_s, q_s, k_s, v_s = _in_proj(xs, mod_s, mod_s, norm_mix_g, w_in_b, qg, kg, cos_s, sin_s, e_mean,
                                  tm=rows, out_dtype=F32, tails=False)
    a_s = _conv_mix(g_s, to_tm(state_conv_a[0]), conv_a_w[0], conv_a_b, ln_a_g, ln_a_b,
                    tc=rows, rs=bd, out_dtype=BF16)
    heads = lambda a: a.reshape(dt, bd, N_HEADS, HEAD_DIM)
    qt = heads(q_s).transpose(1, 3, 0, 2).reshape(bd, HEAD_DIM, dt * N_HEADS).astype(BF16)
    k_new = heads(k_s).transpose(1, 0, 2, 3)
    v_new = heads(v_s).transpose(1, 0, 2, 3)
    o_s, win_k_s, win_v_s = _sample_attn(qt, cache_win_k, cache_win_v, k_new, v_new)
    o_s = o_s.reshape(bd, dt, N_HEADS * HEAD_DIM).transpose(1, 0, 2).reshape(1, rows, D_ATTN)
    x1_s, h2_s = _out_proj(a_s, [o_s], None, xs, mod_s, norm_ffn_g, w_out_b, e16, tm=rows)
    y_s, ftail_s = _conv_ffn(h2_s, x1_s, mod_s, w_up_b, w_down_b, ffn_conv_w[0], ffn_conv_b,
                             to_tm(state_ffn_conv[0]), tm=rows, tf=512, rs=bd)

    from_tm = lambda a, t: a.reshape(t, bd, a.shape[-1]).transpose(1, 0, 2)
    conv_a_s = jnp.concatenate([state_conv_a[0], from_tm(g_s[0], dt)], axis=1)[:, dt:]
    return (y_p, from_tm(y_s[0], dt),
            k_tail[None], v_tail[None],
            g_tail[:, 32 - (CONV_A_WIDTH - 1):][None],
            ftail_p[:, SUBLANES - (FFN_CONV_WIDTH - 1):][None],
            win_k_s, win_v_s,
            conv_a_s[None],
            from_tm(ftail_s[0], FFN_CONV_WIDTH - 1)[None])
```

```python
import functools

import numpy as np
import jax
import jax.numpy as jnp
from jax import lax
from jax.experimental import pallas as pl
from jax.experimental.pallas import tpu as pltpu

F32 = jnp.float32
BF16 = jnp.bfloat16

D_MODEL = 2048
HEAD_DIM = 64
D_CONV = D_MODEL // 2
N_HEADS = (D_MODEL // 2) // HEAD_DIM
D_ATTN = N_HEADS * HEAD_DIM
CONV_A_WIDTH = 31
DIL_PATTERNS = ((128, 1), (512, 4), (2048, 16))
BAND = 128
WIN_MAX = 2048
ROPE_THETA = 10000.0
D_FF = 5632
FFN_CONV_WIDTH = 3
EPS = 1e-6
NEG_INF = -1e30
PAST_LEN = 16384

LANES = 128
SUBLANES = 8
VMEM_LIMIT = 56 * 1024 * 1024
IN_PROJ_VMEM_LIMIT = 60 * 1024 * 1024


def _sigmoid(x):
    return 1.0 / (1.0 + jnp.exp(-x))


def _params(sem, vmem=VMEM_LIMIT):
    return pltpu.CompilerParams(dimension_semantics=sem, vmem_limit_bytes=vmem)


def _ada_kernel(c_ref, w_ref, b_ref, o_ref):
    c = c_ref[...]
    s = (c * _sigmoid(c)).astype(BF16)
    o_ref[...] = jnp.dot(s, w_ref[...].astype(BF16), preferred_element_type=F32) + b_ref[...]


def _ada(c_all, w_ada, b_ada):
    rows, d = c_all.shape
    n = w_ada.shape[1]
    tn = 1024
    return pl.pallas_call(
        _ada_kernel,
        grid=(n // tn,),
        in_specs=[pl.BlockSpec((rows, d), lambda j: (0, 0)),
                  pl.BlockSpec((d, tn), lambda j: (0, j)),
                  pl.BlockSpec((1, tn), lambda j: (0, j))],
        out_specs=pl.BlockSpec((rows, tn), lambda j: (0, j)),
        out_shape=jax.ShapeDtypeStruct((rows, n), F32),
        compiler_params=_params(("parallel",)),
        name="ada_mod",
    )(c_all, w_ada, b_ada)


def _head_norm_rope(x, gain, cos, sin, e):
    ms = jnp.dot((x * x).astype(BF16), e, preferred_element_type=F32)
    xn = x * lax.rsqrt(ms + EPS) * gain
    lane = lax.broadcasted_iota(jnp.int32, (1, LANES), 1)
    first_half = jnp.bitwise_and(lane, HEAD_DIM - 1) < (HEAD_DIM // 2)
    out = []
    for c in range(D_ATTN // LANES):
        xc = xn[:, c * LANES:(c + 1) * LANES]
        partner = jnp.where(first_half, pltpu.roll(xc, LANES - HEAD_DIM // 2, 1),
                            pltpu.roll(xc, HEAD_DIM // 2, 1))
        out.append(xc * cos + partner * sin)
    return out


def _inproj_kernel(x_ref, sc_ref, sh_ref, gmix_ref, w_ref, qg_ref, kg_ref, cos_ref, sin_ref, e_ref,
                   g_ref, q_ref, k_ref, v_ref, *extra_refs, nt, prompt):
    x = x_ref[0]
    ms = jnp.mean(x * x, axis=-1, keepdims=True)
    h = x * lax.rsqrt(ms + EPS) * gmix_ref[...]
    hb = (h * (1.0 + sc_ref[0]) + sh_ref[0]).astype(BF16)

    def proj(j):
        return jnp.dot(hb, w_ref[:, j * D_CONV:(j + 1) * D_CONV], preferred_element_type=F32)

    g = proj(0) * _sigmoid(proj(1))
    g_ref[0] = g.astype(g_ref.dtype)
    cos = cos_ref[...]
    sin = sin_ref[...]
    e = e_ref[...]
    q = jnp.concatenate(_head_norm_rope(proj(2), qg_ref[...], cos, sin, e), axis=1) * (HEAD_DIM ** -0.5)
    k = jnp.concatenate(_head_norm_rope(proj(3), kg_ref[...], cos, sin, e), axis=1)
    v = proj(4)
    qkv = [a.astype(q_ref.dtype) for a in (q, k, v)]
    for ref, a in zip((q_ref, k_ref, v_ref), qkv):
        ref[0] = a

    if prompt:
        view_refs = extra_refs[:6]
        gt_ref, kt_ref, vt_ref = extra_refs[6:]
        tm = x.shape[0]
        for n, (_, dil) in enumerate(DIL_PATTERNS[1:]):
            for ref, a in zip(view_refs[3 * n:3 * n + 3], qkv):
                ref[0] = a.reshape(tm // dil, dil * D_ATTN)
        i = pl.program_id(1)
        rows = gt_ref.shape[1]

        @pl.when(i == nt - 1)
        def _():
            gt_ref[0] = g[g.shape[0] - rows:, :]

        @pl.when(i >= nt // 2)
        def _():
            kt_ref[0] = k.T
            vt_ref[0] = v.T


def _in_proj(x, sc, sh, gmix, w_in, qg, kg, cos, sin, e, *, tm, out_dtype, tails):
    bsz, seq, d = x.shape
    nt = seq // tm
    lm = sc.shape[1]
    mod_rows = tm if lm == seq else 1

    def mod_map(piece):
        if lm == seq:
            return lambda b, i: (b, i, piece)
        return lambda b, i: (b, 0, piece)

    row_spec = pl.BlockSpec((1, tm, D_CONV), lambda b, i: (b, i, 0))
    in_specs = [
        pl.BlockSpec((1, tm, d), lambda b, i: (b, i, 0)),
        pl.BlockSpec((1, mod_rows, d), mod_map(1)),
        pl.BlockSpec((1, mod_rows, d), mod_map(0)),
        pl.BlockSpec((1, d), lambda b, i: (0, 0)),
        pl.BlockSpec(w_in.shape, lambda b, i: (0, 0)),
        pl.BlockSpec((1, D_ATTN), lambda b, i: (0, 0)),
        pl.BlockSpec((1, D_ATTN), lambda b, i: (0, 0)),
        pl.BlockSpec((tm, LANES), lambda b, i: (i, 0)),
        pl.BlockSpec((tm, LANES), lambda b, i: (i, 0)),
        pl.BlockSpec(e.shape, lambda b, i: (0, 0)),
    ]
    out_specs = [row_spec] * 4
    out_shape = [jax.ShapeDtypeStruct((bsz, seq, D_CONV), out_dtype)] * 4
    if tails:
        half = nt // 2
        tail_map = lambda b, i: (b, 0, jnp.maximum(i - half, 0))
        for _, dil in DIL_PATTERNS[1:]:
            out_specs = out_specs + [pl.BlockSpec((1, tm // dil, dil * D_ATTN), lambda b, i: (b, i, 0))] * 3
            out_shape = out_shape + [jax.ShapeDtypeStruct((bsz, seq // dil, dil * D_ATTN), out_dtype)] * 3
        out_specs = out_specs + [pl.BlockSpec((1, 32, D_CONV), lambda b, i: (b, 0, 0)),
                                 pl.BlockSpec((1, D_ATTN, tm), tail_map),
                                 pl.BlockSpec((1, D_ATTN, tm), tail_map)]
        out_shape = out_shape + [jax.ShapeDtypeStruct((bsz, 32, D_CONV), F32),
                                 jax.ShapeDtypeStruct((bsz, D_ATTN, seq // 2), F32),
                                 jax.ShapeDtypeStruct((bsz, D_ATTN, seq // 2), F32)]
    return pl.pallas_call(
        functools.partial(_inproj_kernel, nt=nt, prompt=tails),
        grid=(bsz, nt),
        in_specs=in_specs,
        out_specs=out_specs,
        out_shape=out_shape,
        compiler_params=_params(("arbitrary", "arbitrary"), IN_PROJ_VMEM_LIMIT),
        name="in_proj",
    )(x, sc, sh, gmix, w_in, qg, kg, cos, sin, e)


def _conv_kernel(*refs, tc, rs, halo, has_prev, rc):
    if has_prev:
        g_ref, prev_ref, w_ref, b_ref, lg_ref, lb_ref, o_ref, buf = refs
    else:
        g_ref, w_ref, b_ref, lg_ref, lb_ref, o_ref, buf = refs
    i = pl.program_id(1)

    @pl.when(i == 0)
    def _():
        if has_prev:
            buf[0:halo, :] = prev_ref[0]
        else:
            buf[0:halo, :] = jnp.zeros((halo, D_CONV), F32)

    @pl.when(i > 0)
    def _():
        buf[0:halo, :] = buf[tc:tc + halo, :]

    buf[halo:halo + tc, :] = g_ref[0].astype(F32)
    base = halo - (CONV_A_WIDTH - 1) * rs
    bias = b_ref[...]
    lg = lg_ref[...]
    lb = lb_ref[...]
    for c in range(tc // rc):
        r0 = c * rc
        acc = jnp.broadcast_to(bias, (rc, D_CONV))
        if rs % SUBLANES == 0:
            for k in range(CONV_A_WIDTH):
                off = r0 + base + k * rs
                acc = acc + w_ref[k:k + 1, :] * buf[off:off + rc, :]
        else:
            n = rc + halo
            ext = buf[r0:r0 + n, :]
            for s in range(SUBLANES):
                rolled = ext if s == 0 else pltpu.roll(ext, n - s, 0)
                for k in range(CONV_A_WIDTH):
                    off = base + k * rs
                    if off % SUBLANES == s:
                        acc = acc + w_ref[k:k + 1, :] * rolled[off - s:off - s + rc, :]
        mu = jnp.mean(acc, axis=-1, keepdims=True)
        xc = acc - mu
        var = jnp.mean(xc * xc, axis=-1, keepdims=True)
        y = xc * lax.rsqrt(var + EPS) * lg + lb
        o_ref[0, r0:r0 + rc, :] = (y * _sigmoid(y)).astype(o_ref.dtype)


def _conv_mix(g, prev, w, b, lg, lb, *, tc, rs, out_dtype):
    bsz, seq, ch = g.shape
    has_prev = prev is not None
    halo = prev.shape[1] if has_prev else 32
    rc = min(tc, 32)
    in_specs = [pl.BlockSpec((1, tc, ch), lambda bb, i: (bb, i, 0))]
    args = [g]
    if has_prev:
        in_specs.append(pl.BlockSpec((1, halo, ch), lambda bb, i: (bb, 0, 0)))
        args.append(prev)
    in_specs += [pl.BlockSpec(w.shape, lambda bb, i: (0, 0))] + [pl.BlockSpec((1, ch), lambda bb, i: (0, 0))] * 3
    args += [w, b, lg, lb]
    return pl.pallas_call(
        functools.partial(_conv_kernel, tc=tc, rs=rs, halo=halo, has_prev=has_prev, rc=rc),
        grid=(bsz, seq // tc),
        in_specs=in_specs,
        out_specs=pl.BlockSpec((1, tc, ch), lambda bb, i: (bb, i, 0)),
        out_shape=jax.ShapeDtypeStruct((bsz, seq, ch), out_dtype),
        scratch_shapes=[pltpu.VMEM((halo + tc, ch), F32)],
        compiler_params=_params(("arbitrary", "arbitrary")),
        name="conv_mix",
    )(*args)


def _band_attn_kernel(q_ref, k_ref, v_ref, o_ref, lse_ref, *, lr, hb, ub):
    nblk = lr // BAND
    qi = lax.broadcasted_iota(jnp.int32, (BAND, 2 * BAND), 0)
    kj = lax.broadcasted_iota(jnp.int32, (BAND, 2 * BAND), 1)
    bias_band = jnp.where(kj >= qi, jnp.where(kj <= qi + BAND, 0.0, NEG_INF), NEG_INF).astype(F32)
    qi1 = lax.broadcasted_iota(jnp.int32, (BAND, BAND), 0)
    kj1 = lax.broadcasted_iota(jnp.int32, (BAND, BAND), 1)
    bias_first = jnp.where(kj1 <= qi1, 0.0, NEG_INF).astype(F32)

    def block(q0, k0, nk, bias):
        for hh in range(hb):
            lo = hh * HEAD_DIM
            qb = q_ref[0, pl.ds(q0, BAND), lo:lo + HEAD_DIM]
            kb = k_ref[0, pl.ds(k0, nk), lo:lo + HEAD_DIM]
            vb = v_ref[0, pl.ds(k0, nk), lo:lo + HEAD_DIM]
            s = lax.dot_general(qb, kb, (((1,), (1,)), ((), ())), preferred_element_type=F32) + bias
            m = jnp.max(s, axis=-1, keepdims=True)
            p = jnp.exp(s - m)
            l = jnp.sum(p, axis=-1, keepdims=True)
            o = jnp.dot(p.astype(BF16), vb, preferred_element_type=F32) * (1.0 / l)
            o_ref[0, pl.ds(q0, BAND), lo:lo + HEAD_DIM] = o.astype(o_ref.dtype)
            lse_ref[0, 0, 0, pl.ds(q0, BAND), hh:hh + 1] = m + jnp.log(l)

    block(0, 0, BAND, bias_first)

    def banded(i):
        block(pl.multiple_of(i * BAND, BAND), pl.multiple_of((i - 1) * BAND, BAND), 2 * BAND, bias_band)

    groups = (nblk - 1) // ub

    def body(gi, carry):
        for j in range(ub):
            banded(1 + gi * ub + j)
        return carry

    lax.fori_loop(0, groups, body, 0)
    for i in range(1 + groups * ub, nblk):
        banded(i)


def _band_attn(q, k, v, dil, width, ub):
    bsz, lr, cols = q.shape
    da = cols // dil
    seq = lr * dil
    hb = width // HEAD_DIM
    ncol = da // width
    spec = pl.BlockSpec((1, lr, width), lambda b, c: (b, 0, c))
    o, lse = pl.pallas_call(
        functools.partial(_band_attn_kernel, lr=lr, hb=hb, ub=ub),
        grid=(bsz, dil * ncol),
        in_specs=[spec, spec, spec],
        out_specs=[spec,
                   pl.BlockSpec((1, 1, 1, lr, hb), lambda b, c: (b, c // ncol, c % ncol, 0, 0))],
        out_shape=[jax.ShapeDtypeStruct((bsz, lr, cols), BF16),
                   jax.ShapeDtypeStruct((bsz, dil, ncol, lr, hb), F32)],
        compiler_params=_params(("parallel", "parallel")),
        name=f"band_attn_d{dil}",
    )(q, k, v)
    lse = lse.transpose(0, 3, 1, 2, 4).reshape(bsz, seq, N_HEADS)
    return o, lse


DEC_T = 4
HEAD_GROUP = 4
GROUP_COLS = HEAD_GROUP * HEAD_DIM
NEW_LANE0 = LANES - DEC_T


def _sample_mult_tables():
    t = np.arange(HEAD_GROUP * DEC_T) % DEC_T

    def table(dist):
        m = np.zeros(dist.shape, np.float32)
        for window, dil in DIL_PATTERNS:
            m += ((dist >= 0) & (dist <= window) & (dist % dil == 0)).astype(np.float32)
        return m

    win = table(WIN_MAX + t[:, None] - np.arange(WIN_MAX)[None, :])
    lane = np.arange(LANES)
    new = table(np.where(lane >= NEW_LANE0, t[:, None] - (lane[None, :] - NEW_LANE0), -1))
    row_head = np.arange(HEAD_GROUP * DEC_T) // DEC_T
    own = (row_head[:, None] == (np.arange(GROUP_COLS) // HEAD_DIM)[None, :]).astype(np.float32)
    return win, new, own


def _sattn_kernel(q_ref, kt_ref, vt_ref, kn_ref, vn_ref, mw_ref, mn_ref, own_ref, o_ref, okt_ref, ovt_ref):
    q = q_ref[...]
    kt = kt_ref[...]
    kn = kn_ref[...]
    mw = mw_ref[...]
    mn = mn_ref[...]
    s_win = jnp.where(mw > 0.0, jnp.dot(q, kt.astype(BF16), preferred_element_type=F32), NEG_INF)
    s_new = jnp.where(mn > 0.0, jnp.dot(q, kn.astype(BF16), preferred_element_type=F32), NEG_INF)
    m = jnp.maximum(jnp.max(s_win, axis=-1, keepdims=True), jnp.max(s_new, axis=-1, keepdims=True))
    p_win = mw * jnp.exp(s_win - m)
    p_new = mn * jnp.exp(s_new - m)
    den = jnp.sum(p_win, axis=-1, keepdims=True) + jnp.sum(p_new, axis=-1, keepdims=True)
    vt = vt_ref[...]
    vn = vn_ref[...]
    contract_lanes = (((1,), (1,)), ((), ()))
    o = (lax.dot_general(p_win.astype(BF16), vt.astype(BF16), contract_lanes, preferred_element_type=F32)
         + lax.dot_general(p_new.astype(BF16), vn.astype(BF16), contract_lanes, preferred_element_type=F32))
    o = o * (1.0 / den) * own_ref[...]
    o_ref[...] = functools.reduce(lambda a, c: a + c,
                                  [o[h * DEC_T:(h + 1) * DEC_T] for h in range(HEAD_GROUP)])

    lane = lax.broadcasted_iota(jnp.int32, kn.shape, 1)
    last = WIN_MAX - LANES
    for src, new, dst in ((kt, kn, okt_ref), (vt, vn, ovt_ref)):
        rolled = pltpu.roll(src, WIN_MAX - DEC_T, 1)
        dst[...] = rolled
        dst[:, last:] = jnp.where(lane >= NEW_LANE0, new, rolled[:, last:])


def _sample_attn(q_blk, kt, vt, kn, vn):
    bd = kt.shape[0]
    ng = N_HEADS // HEAD_GROUP
    win, new, own = (jnp.asarray(t) for t in _sample_mult_tables())
    win_spec = pl.BlockSpec((None, GROUP_COLS, WIN_MAX), lambda b, g: (b, g, 0))
    new_spec = pl.BlockSpec((None, GROUP_COLS, LANES), lambda b, g: (b, g, 0))
    const = lambda a: pl.BlockSpec(a.shape, lambda b, g: (0, 0))
    return pl.pallas_call(
        _sattn_kernel,
        grid=(bd, ng),
        in_specs=[pl.BlockSpec((None, None, HEAD_GROUP * DEC_T, GROUP_COLS), lambda b, g: (b, g, 0, 0)),
                  win_spec, win_spec, new_spec, new_spec, const(win), const(new), const(own)],
        out_specs=[pl.BlockSpec((None, DEC_T, GROUP_COLS), lambda b, g: (b, 0, g)), win_spec, win_spec],
        out_shape=[jax.ShapeDtypeStruct((bd, DEC_T, D_ATTN), F32),
                   jax.ShapeDtypeStruct(kt.shape, kt.dtype),
                   jax.ShapeDtypeStruct(vt.shape, vt.dtype)],
        compiler_params=_params(("parallel", "parallel")),
        name="sample_attn",
    )(q_blk, kt, vt, kn, vn, win, new, own)


def _outproj_kernel(*refs, n_pat):
    a_ref = refs[0]
    o_refs = refs[1:1 + n_pat]
    pos = 1 + n_pat
    lse_refs = ()
    if n_pat > 1:
        lse_refs = refs[pos:pos + n_pat]
        pos += n_pat
    x_ref, g1_ref, sc_ref, sh_ref, gn_ref, w_ref, e_ref, x1_ref, h2_ref = refs[pos:]

    if n_pat > 1:
        lses = [r[0] for r in lse_refs]
        mx = functools.reduce(jnp.maximum, lses)
        es = [jnp.exp(l - mx) for l in lses]
        inv = 1.0 / functools.reduce(lambda a, c: a + c, es)
        e16 = e_ref[...]
        o = None
        for ep, o_ref in zip(es, o_refs):
            wp = ep * inv
            hi = wp.astype(BF16)
            lo = (wp - hi.astype(F32)).astype(BF16)
            wb = (jnp.dot(hi, e16, preferred_element_type=F32)
                  + jnp.dot(lo, e16, preferred_element_type=F32))
            term = wb * o_ref[0].reshape(wb.shape).astype(F32)
            o = term if o is None else o + term
        ob = o.astype(BF16)
    else:
        ob = o_refs[0][0].astype(BF16)

    mix = (jnp.dot(a_ref[0].astype(BF16), w_ref[0:D_CONV, :], preferred_element_type=F32)
           + jnp.dot(ob, w_ref[D_CONV:, :], preferred_element_type=F32))
    x1 = x_ref[0] + g1_ref[0] * mix
    x1_ref[0] = x1
    ms = jnp.mean(x1 * x1, axis=-1, keepdims=True)
    h2 = x1 * lax.rsqrt(ms + EPS) * gn_ref[...]
    h2_ref[0] = (h2 * (1.0 + sc_ref[0]) + sh_ref[0]).astype(h2_ref.dtype)


def _out_proj(a_out, outs, lses, x, mod, gn, w_out, e16, *, tm):
    bsz, seq, d = x.shape
    n_pat = len(outs)
    lm = mod.shape[1]
    mod_rows = tm if lm == seq else 1

    def mod_map(piece):
        if lm == seq:
            return lambda b, i: (b, i, piece)
        return lambda b, i: (b, 0, piece)

    half_spec = pl.BlockSpec((1, tm, D_CONV), lambda b, i: (b, i, 0))
    full_spec = pl.BlockSpec((1, tm, d), lambda b, i: (b, i, 0))
    in_specs = [half_spec]
    for o in outs:
        dil = seq // o.shape[1]
        in_specs.append(pl.BlockSpec((1, tm // dil, dil * D_ATTN), lambda b, i: (b, i, 0)))
    args = [a_out, *outs]
    if n_pat > 1:
        in_specs += [pl.BlockSpec((1, tm, N_HEADS), lambda b, i: (b, i, 0))] * n_pat
        args += list(lses)
    in_specs += [full_spec,
                 pl.BlockSpec((1, mod_rows, d), mod_map(2)),
                 pl.BlockSpec((1, mod_rows, d), mod_map(4)),
                 pl.BlockSpec((1, mod_rows, d), mod_map(3)),
                 pl.BlockSpec((1, d), lambda b, i: (0, 0)),
                 pl.BlockSpec(w_out.shape, lambda b, i: (0, 0)),
                 pl.BlockSpec(e16.shape, lambda b, i: (0, 0))]
    args += [x, mod, mod, mod, gn, w_out, e16]
    return pl.pallas_call(
        functools.partial(_outproj_kernel, n_pat=n_pat),
        grid=(bsz, seq // tm),
        in_specs=in_specs,
        out_specs=[full_spec, full_spec],
        out_shape=[jax.ShapeDtypeStruct((bsz, seq, d), F32), jax.ShapeDtypeStruct((bsz, seq, d), BF16)],
        compiler_params=_params(("parallel", "parallel")),
        name="out_proj",
    )(*args)


def _ffn_kernel(*refs, tm, rs, halo, has_prev, nf):
    (h2_ref, x1_ref, g2_ref, wg_ref, wv_ref, wd_ref, cwg_ref, cwv_ref, cbg_ref, cbv_ref) = refs[:10]
    pos = 10
    prev_refs = (None, None)
    if has_prev:
        prev_refs = refs[pos:pos + 2]
        pos += 2
    y_ref, tg_ref, tv_ref, acc_ref, ug_buf, uv_buf, carry_ref = refs[pos:]
    m = pl.program_id(1)
    f = pl.program_id(2)
    h2 = h2_ref[0]

    def conv_half(w_ref, cw_ref, cb_ref, prev_ref, ubuf, t_ref, slot):
        u = jnp.dot(h2, w_ref[...], preferred_element_type=F32)

        @pl.when(m == 0)
        def _():
            if has_prev:
                ubuf[0:halo, :] = prev_ref[0]
            else:
                ubuf[0:halo, :] = jnp.zeros((halo, u.shape[1]), F32)

        @pl.when(m > 0)
        def _():
            ubuf[0:halo, :] = carry_ref[f, slot]

        ubuf[halo:halo + tm, :] = u
        last = u[tm - halo:, :]
        carry_ref[f, slot] = last
        t_ref[0] = last
        return (cb_ref[...] + cw_ref[0:1, :] * ubuf[halo - 2 * rs:halo - 2 * rs + tm, :]
                + cw_ref[1:2, :] * ubuf[halo - rs:halo - rs + tm, :] + cw_ref[2:3, :] * u)

    gate = conv_half(wg_ref, cwg_ref, cbg_ref, prev_refs[0], ug_buf, tg_ref, 0)
    val = conv_half(wv_ref, cwv_ref, cbv_ref, prev_refs[1], uv_buf, tv_ref, 1)
    act = (gate * _sigmoid(gate) * val).astype(BF16)
    contrib = jnp.dot(act, wd_ref[...], preferred_element_type=F32)

    @pl.when(f == 0)
    def _():
        acc_ref[...] = contrib

    @pl.when(f > 0)
    def _():
        acc_ref[...] += contrib

    @pl.when(f == nf - 1)
    def _():
        y_ref[0] = x1_ref[0] + g2_ref[0] * acc_ref[...]


def _conv_ffn(h2, x1, mod, w_up, w_down, cw, cb, prev, *, tm, tf, rs):
    bsz, seq, d = x1.shape
    nf = D_FF // tf
    has_prev = prev is not None
    halo = 2 * rs if has_prev else SUBLANES
    lm = mod.shape[1]
    mod_rows = tm if lm == seq else 1
    g2_map = (lambda b, m, f: (b, m, 5)) if lm == seq else (lambda b, m, f: (b, 0, 5))
    row = lambda b, m, f: (b, m, 0)
    gate_col = lambda b, m, f: (0, f)
    val_col = lambda b, m, f: (0, nf + f)
    in_specs = [pl.BlockSpec((1, tm, d), row), pl.BlockSpec((1, tm, d), row),
                pl.BlockSpec((1, mod_rows, d), g2_map),
                pl.BlockSpec((d, tf), gate_col), pl.BlockSpec((d, tf), val_col),
                pl.BlockSpec((tf, d), lambda b, m, f: (f, 0)),
                pl.BlockSpec((FFN_CONV_WIDTH, tf), gate_col), pl.BlockSpec((FFN_CONV_WIDTH, tf), val_col),
                pl.BlockSpec((1, tf), gate_col), pl.BlockSpec((1, tf), val_col)]
    args = [h2, x1, mod, w_up, w_up, w_down, cw, cw, cb, cb]
    if has_prev:
        in_specs += [pl.BlockSpec((1, halo, tf), lambda b, m, f: (b, 0, f)),
                     pl.BlockSpec((1, halo, tf), lambda b, m, f: (b, 0, nf + f))]
        args += [prev, prev]
    nm = seq // tm
    tail_spec = pl.BlockSpec((1, halo, tf), lambda b, m, f: (b, m, f))
    tail_shape = jax.ShapeDtypeStruct((bsz, nm * halo, D_FF), F32)
    y, tail_g, tail_v = pl.pallas_call(
        functools.partial(_ffn_kernel, tm=tm, rs=rs, halo=halo, has_prev=has_prev, nf=nf),
        grid=(bsz, nm, nf),
        in_specs=in_specs,
        out_specs=[pl.BlockSpec((1, tm, d), row), tail_spec, tail_spec],
        out_shape=[jax.ShapeDtypeStruct((bsz, seq, d), F32), tail_shape, tail_shape],
        scratch_shapes=[pltpu.VMEM((tm, d), F32),
                        pltpu.VMEM((halo + tm, tf), F32), pltpu.VMEM((halo + tm, tf), F32),
                        pltpu.VMEM((nf, 2, halo, tf), F32)],
        compiler_params=_params(("arbitrary", "arbitrary", "arbitrary")),
        name="conv_ffn",
    )(*args)
    last = (nm - 1) * halo
    return y, jnp.concatenate([tail_g[:, last:], tail_v[:, last:]], axis=-1)


def _ffn_pipe_kernel(h2_ref, x1_ref, g2_ref, wg_ref, wv_ref, wd_ref, cwg_ref, cwv_ref, cbg_ref, cbv_ref,
                     y_ref, tg_ref, tv_ref, acc_ref, ua_ref, ub_ref, carry_ref, *, tm, nf):
    m = pl.program_id(1)
    f = pl.program_id(2)
    fb = jnp.maximum(f - 1, 0)

    @pl.when(f == 0)
    def _():
        ub_ref[...] = jnp.zeros(ub_ref.shape, F32)

        @pl.when(m == 0)
        def _():
            carry_ref[...] = jnp.zeros(carry_ref.shape, F32)

    def step(u_rd, u_wr):
        h2 = h2_ref[0]
        u_wr[0] = jnp.dot(h2, wg_ref[...], preferred_element_type=F32)
        u_wr[1] = jnp.dot(h2, wv_ref[...], preferred_element_type=F32)

        def conv_half(slot, cw_ref, cb_ref, t_ref):
            u = u_rd[slot]
            kept = carry_ref[fb, slot]
            prev = jnp.where(m == 0, 0.0, kept)
            last = u[tm - SUBLANES:, :]
            carry_ref[fb, slot] = jnp.where(f == 0, kept, last)
            t_ref[0] = last
            ext = jnp.concatenate([prev, u], axis=0)
            u1 = pltpu.roll(ext, 1, 0)[SUBLANES:]
            u2 = pltpu.roll(ext, 2, 0)[SUBLANES:]
            return cb_ref[...] + cw_ref[0:1, :] * u2 + cw_ref[1:2, :] * u1 + cw_ref[2:3, :] * u

        gate = conv_half(0, cwg_ref, cbg_ref, tg_ref)
        val = conv_half(1, cwv_ref, cbv_ref, tv_ref)
        act = (gate * _sigmoid(gate) * val).astype(BF16)
        contrib = jnp.dot(act, wd_ref[...], preferred_element_type=F32)
        acc_ref[...] = jnp.where(f <= 1, 0.0, acc_ref[...]) + contrib

    @pl.when(lax.rem(f, 2) == 0)
    def _():
        step(ub_ref, ua_ref)

    @pl.when(lax.rem(f, 2) == 1)
    def _():
        step(ua_ref, ub_ref)

    @pl.when(f == nf)
    def _():
        y_ref[0] = x1_ref[0] + g2_ref[0] * acc_ref[...]


def _conv_ffn_prompt(h2, x1, mod, w_up, w_down, cw, cb, *, tm, tf):
    bsz, seq, d = x1.shape
    nf = D_FF // tf
    nm = seq // tm
    row = lambda b, m, f: (b, m, 0)
    up = lambda f: jnp.minimum(f, nf - 1)
    down = lambda f: jnp.maximum(f - 1, 0)
    in_specs = [pl.BlockSpec((1, tm, d), row), pl.BlockSpec((1, tm, d), row),
                pl.BlockSpec((1, 1, d), lambda b, m, f: (b, 0, 5)),
                pl.BlockSpec((d, tf), lambda b, m, f: (0, up(f))),
                pl.BlockSpec((d, tf), lambda b, m, f: (0, nf + up(f))),
                pl.BlockSpec((tf, d), lambda b, m, f: (down(f), 0)),
                pl.BlockSpec((FFN_CONV_WIDTH, tf), lambda b, m, f: (0, down(f))),
                pl.BlockSpec((FFN_CONV_WIDTH, tf), lambda b, m, f: (0, nf + down(f))),
                pl.BlockSpec((1, tf), lambda b, m, f: (0, down(f))),
                pl.BlockSpec((1, tf), lambda b, m, f: (0, nf + down(f)))]
    tail_spec = pl.BlockSpec((1, SUBLANES, tf), lambda b, m, f: (b, m, down(f)))
    tail_shape = jax.ShapeDtypeStruct((bsz, nm * SUBLANES, D_FF), F32)
    y, tail_g, tail_v = pl.pallas_call(
        functools.partial(_ffn_pipe_kernel, tm=tm, nf=nf),
        grid=(bsz, nm, nf + 1),
        in_specs=in_specs,
        out_specs=[pl.BlockSpec((1, tm, d), row), tail_spec, tail_spec],
        out_shape=[jax.ShapeDtypeStruct((bsz, seq, d), F32), tail_shape, tail_shape],
        scratch_shapes=[pltpu.VMEM((tm, d), F32),
                        pltpu.VMEM((2, tm, tf), F32), pltpu.VMEM((2, tm, tf), F32),
                        pltpu.VMEM((nf, 2, SUBLANES, tf), F32)],
        compiler_params=_params(("arbitrary", "arbitrary", "arbitrary")),
        name="conv_ffn_pipe",
    )(h2, x1, mod, w_up, w_up, w_down, cw, cw, cb, cb)
    last = (nm - 1) * SUBLANES
    return y, jnp.concatenate([tail_g[:, last:], tail_v[:, last:]], axis=-1)


def _rope_tables(pos):
    half = HEAD_DIM // 2
    inv = ROPE_THETA ** (-jnp.arange(half, dtype=F32) / half)
    ang = pos.astype(F32)[:, None] * inv[None, :]
    cos = jnp.cos(ang)
    sin = jnp.sin(ang)
    cos_l = jnp.concatenate([cos, cos, cos, cos], axis=-1)
    sin_l = jnp.concatenate([-sin, sin, -sin, sin], axis=-1)
    return cos_l, sin_l


def _head_mean_matrix():
    h = np.arange(D_ATTN) // HEAD_DIM
    return jnp.asarray((h[:, None] == h[None, :]).astype(np.float32) / HEAD_DIM, dtype=BF16)


def _head_expand_matrix():
    h = np.arange(D_ATTN) // HEAD_DIM
    return jnp.asarray((np.arange(N_HEADS)[:, None] == h[None, :]).astype(np.float32), dtype=BF16)


def kernel(x_prompt, x_sample, cache_win_k, cache_win_v, state_conv_a, state_ffn_conv, c_prompt, c_sample,
           norm_mix_g, norm_ffn_g, w_ada, b_ada, w_in, conv_a_w, conv_a_b, ln_a_g, ln_a_b, q_norm_g, k_norm_g,
           w_out, w_up, ffn_conv_w, ffn_conv_b, w_down):
    bsz, seq, d = x_prompt.shape
    bd, dt, _ = x_sample.shape
    assert w_ada.shape[0] == 1 and dt == DEC_T and cache_win_k.shape[2] == WIN_MAX and seq == 2 * WIN_MAX

    w_in_b = w_in[0].astype(BF16)
    w_out_b = w_out[0].astype(BF16)
    w_up_b = w_up[0].astype(BF16)
    w_down_b = w_down[0].astype(BF16)
    e_mean = _head_mean_matrix()
    e16 = _head_expand_matrix()
    qg = jnp.tile(q_norm_g[0], N_HEADS)[None, :]
    kg = jnp.tile(k_norm_g[0], N_HEADS)[None, :]

    c_all = jnp.concatenate([c_prompt, c_sample, jnp.zeros((16 - bsz - bd, d), F32)], axis=0)
    mod = _ada(c_all, w_ada[0], b_ada)
    mod_p = mod[:bsz][:, None, :]
    mod_s = jnp.tile(mod[bsz:bsz + bd], (dt, 1))[None]

    cos_p, sin_p = _rope_tables(jnp.arange(seq))
    g_p, *qkv_views, g_tail, k_tail, v_tail = _in_proj(
        x_prompt, mod_p, mod_p, norm_mix_g, w_in_b, qg, kg, cos_p, sin_p, e_mean,
        tm=256, out_dtype=BF16, tails=True)
    a_p = _conv_mix(g_p, None, conv_a_w[0], conv_a_b, ln_a_g, ln_a_b, tc=256, rs=1, out_dtype=BF16)
    outs, lses = [], []
    for n, ((window, dil), width, ub) in enumerate(zip(DIL_PATTERNS, (256, 512, 1024), (2, 1, 1))):
        o, lse = _band_attn(*qkv_views[3 * n:3 * n + 3], dil, width, ub)
        outs.append(o)
        lses.append(lse)
    x1_p, h2_p = _out_proj(a_p, outs, lses, x_prompt, mod_p, norm_ffn_g, w_out_b, e16, tm=256)
    y_p, ftail_p = _conv_ffn_prompt(h2_p, x1_p, mod_p, w_up_b, w_down_b, ffn_conv_w[0], ffn_conv_b,
                                    tm=512, tf=512)

    rows = dt * bd
    to_tm = lambda a: a.transpose(1, 0, 2).reshape(1, a.shape[0] * a.shape[1], a.shape[2])
    xs = to_tm(x_sample)
    cos_s, sin_s = _rope_tables(PAST_LEN + jnp.arange(rows) // bd)
    g_s, q_s, k_s, v_s = _in_proj(xs, mod_s, mod_s, norm_mix_g, w_in_b, qg, kg, cos_s, sin_s, e_mean,
                                  tm=rows, out_dtype=F32, tails=False)
    a_s = _conv_mix(g_s, to_tm(state_conv_a[0]), conv_a_w[0], conv_a_b, ln_a_g, ln_a_b,
                    tc=rows, rs=bd, out_dtype=BF16)
    heads = lambda a: a.reshape(dt, bd, N_HEADS, HEAD_DIM)
    ng = N_HEADS // HEAD_GROUP
    qg5 = heads(q_s).reshape(dt, bd, ng, HEAD_GROUP, HEAD_DIM).transpose(1, 2, 3, 0, 4)
    q_blk = (qg5[:, :, :, :, None, :] * jnp.eye(HEAD_GROUP, dtype=F32)[None, None, :, None, :, None])
    q_blk = q_blk.reshape(bd, ng, HEAD_GROUP * dt, GROUP_COLS).astype(BF16)
    to_pos_minor = lambda c: c[0].transpose(0, 2, 3, 1).reshape(bd, D_ATTN, WIN_MAX)
    new_tile = lambda a: jnp.pad(heads(a).transpose(1, 2, 3, 0).reshape(bd, D_ATTN, dt),
                                 ((0, 0), (0, 0), (NEW_LANE0, 0)))
    o_s, win_kt, win_vt = _sample_attn(q_blk, to_pos_minor(cache_win_k), to_pos_minor(cache_win_v),
                                       new_tile(k_s), new_tile(v_s))
    o_s = o_s.transpose(1, 0, 2).reshape(1, rows, D_ATTN)
    from_pos_minor = lambda a: a.reshape(a.shape[0], N_HEADS, HEAD_DIM, a.shape[-1]).transpose(0, 3, 1, 2)[None]
    win_k_s, win_v_s = from_pos_minor(win_kt), from_pos_minor(win_vt)
    x1_s, h2_s = _out_proj(a_s, [o_s], None, xs, mod_s, norm_ffn_g, w_out_b, e16, tm=rows)
    y_s, ftail_s = _conv_ffn(h2_s, x1_s, mod_s, w_up_b, w_down_b, ffn_conv_w[0], ffn_conv_b,
                             to_tm(state_ffn_conv[0]), tm=rows, tf=512, rs=bd)

    from_tm = lambda a, t: a.reshape(t, bd, a.shape[-1]).transpose(1, 0, 2)
    conv_a_s = jnp.concatenate([state_conv_a[0], from_tm(g_s[0], dt)], axis=1)[:, dt:]
    return (y_p, from_tm(y_s[0], dt),
            from_pos_minor(k_tail), from_pos_minor(v_tail),
            g_tail[:, 32 - (CONV_A_WIDTH - 1):][None],
            ftail_p[:, SUBLANES - (FFN_CONV_WIDTH - 1):][None],
            win_k_s, win_v_s,
            conv_a_s[None],
            from_tm(ftail_s[0], FFN_CONV_WIDTH - 1)[None])
```

```python
import functools

import numpy as np
import jax
import jax.numpy as jnp
from jax import lax
from jax.experimental import pallas as pl
from jax.experimental.pallas import tpu as pltpu

F32 = jnp.float32
BF16 = jnp.bfloat16

D_MODEL = 2048
HEAD_DIM = 64
D_CONV = D_MODEL // 2
N_HEADS = (D_MODEL // 2) // HEAD_DIM
D_ATTN = N_HEADS * HEAD_DIM
CONV_A_WIDTH = 31
DIL_PATTERNS = ((128, 1), (512, 4), (2048, 16))
BAND = 128
WIN_MAX = 2048
ROPE_THETA = 10000.0
D_FF = 5632
FFN_CONV_WIDTH = 3
EPS = 1e-6
NEG_INF = -1e30
PAST_LEN = 16384

LANES = 128
SUBLANES = 8
VMEM_LIMIT = 56 * 1024 * 1024
IN_PROJ_VMEM_LIMIT = 60 * 1024 * 1024


def _sigmoid(x):
    return 1.0 / (1.0 + jnp.exp(-x))


def _params(sem, vmem=VMEM_LIMIT):
    return pltpu.CompilerParams(dimension_semantics=sem, vmem_limit_bytes=vmem)


def _ada_kernel(c_ref, w_ref, b_ref, o_ref):
    c = c_ref[...]
    s = (c * _sigmoid(c)).astype(BF16)
    o_ref[...] = jnp.dot(s, w_ref[...].astype(BF16), preferred_element_type=F32) + b_ref[...]


def _ada(c_all, w_ada, b_ada):
    rows, d = c_all.shape
    n = w_ada.shape[1]
    tn = 1024
    return pl.pallas_call(
        _ada_kernel,
        grid=(n // tn,),
        in_specs=[pl.BlockSpec((rows, d), lambda j: (0, 0)),
                  pl.BlockSpec((d, tn), lambda j: (0, j)),
                  pl.BlockSpec((1, tn), lambda j: (0, j))],
        out_specs=pl.BlockSpec((rows, tn), lambda j: (0, j)),
        out_shape=jax.ShapeDtypeStruct((rows, n), F32),
        compiler_params=_params(("parallel",)),
        name="ada_mod",
    )(c_all, w_ada, b_ada)


def _head_norm_rope(x, gain, cos, sin, e):
    ms = jnp.dot((x * x).astype(BF16), e, preferred_element_type=F32)
    xn = x * lax.rsqrt(ms + EPS) * gain
    lane = lax.broadcasted_iota(jnp.int32, (1, LANES), 1)
    first_half = jnp.bitwise_and(lane, HEAD_DIM - 1) < (HEAD_DIM // 2)
    out = []
    for c in range(D_ATTN // LANES):
        xc = xn[:, c * LANES:(c + 1) * LANES]
        partner = jnp.where(first_half, pltpu.roll(xc, LANES - HEAD_DIM // 2, 1),
                            pltpu.roll(xc, HEAD_DIM // 2, 1))
        out.append(xc * cos + partner * sin)
    return out


def _inproj_kernel(x_ref, sc_ref, sh_ref, gmix_ref, w_ref, qg_ref, kg_ref, cos_ref, sin_ref, e_ref,
                   g_ref, q_ref, k_ref, v_ref, *extra_refs, nt, prompt):
    x = x_ref[0]
    ms = jnp.mean(x * x, axis=-1, keepdims=True)
    h = x * lax.rsqrt(ms + EPS) * gmix_ref[...]
    hb = (h * (1.0 + sc_ref[0]) + sh_ref[0]).astype(BF16)

    def proj(j):
        return jnp.dot(hb, w_ref[:, j * D_CONV:(j + 1) * D_CONV], preferred_element_type=F32)

    g = proj(0) * _sigmoid(proj(1))
    g_ref[0] = g.astype(g_ref.dtype)
    cos = cos_ref[...]
    sin = sin_ref[...]
    e = e_ref[...]
    q = jnp.concatenate(_head_norm_rope(proj(2), qg_ref[...], cos, sin, e), axis=1) * (HEAD_DIM ** -0.5)
    k = jnp.concatenate(_head_norm_rope(proj(3), kg_ref[...], cos, sin, e), axis=1)
    v = proj(4)
    qkv = [a.astype(q_ref.dtype) for a in (q, k, v)]
    for ref, a in zip((q_ref, k_ref, v_ref), qkv):
        ref[0] = a

    if prompt:
        view_refs = extra_refs[:6]
        gt_ref, kt_ref, vt_ref = extra_refs[6:]
        tm = x.shape[0]
        for n, (_, dil) in enumerate(DIL_PATTERNS[1:]):
            for ref, a in zip(view_refs[3 * n:3 * n + 3], qkv):
                ref[0] = a.reshape(tm // dil, dil * D_ATTN)
        i = pl.program_id(1)
        rows = gt_ref.shape[1]

        @pl.when(i == nt - 1)
        def _():
            gt_ref[0] = g[g.shape[0] - rows:, :]

        @pl.when(i >= nt // 2)
        def _():
            kt_ref[0] = k.T
            vt_ref[0] = v.T


def _in_proj(x, sc, sh, gmix, w_in, qg, kg, cos, sin, e, *, tm, out_dtype, tails):
    bsz, seq, d = x.shape
    nt = seq // tm
    lm = sc.shape[1]
    mod_rows = tm if lm == seq else 1

    def mod_map(piece):
        if lm == seq:
            return lambda b, i: (b, i, piece)
        return lambda b, i: (b, 0, piece)

    row_spec = pl.BlockSpec((1, tm, D_CONV), lambda b, i: (b, i, 0))
    in_specs = [
        pl.BlockSpec((1, tm, d), lambda b, i: (b, i, 0)),
        pl.BlockSpec((1, mod_rows, d), mod_map(1)),
        pl.BlockSpec((1, mod_rows, d), mod_map(0)),
        pl.BlockSpec((1, d), lambda b, i: (0, 0)),
        pl.BlockSpec(w_in.shape, lambda b, i: (0, 0)),
        pl.BlockSpec((1, D_ATTN), lambda b, i: (0, 0)),
        pl.BlockSpec((1, D_ATTN), lambda b, i: (0, 0)),
        pl.BlockSpec((tm, LANES), lambda b, i: (i, 0)),
        pl.BlockSpec((tm, LANES), lambda b, i: (i, 0)),
        pl.BlockSpec(e.shape, lambda b, i: (0, 0)),
    ]
    out_specs = [row_spec] * 4
    out_shape = [jax.ShapeDtypeStruct((bsz, seq, D_CONV), out_dtype)] * 4
    if tails:
        half = nt // 2
        tail_map = lambda b, i: (b, 0, jnp.maximum(i - half, 0))
        for _, dil in DIL_PATTERNS[1:]:
            out_specs = out_specs + [pl.BlockSpec((1, tm // dil, dil * D_ATTN), lambda b, i: (b, i, 0))] * 3
            out_shape = out_shape + [jax.ShapeDtypeStruct((bsz, seq // dil, dil * D_ATTN), out_dtype)] * 3
        out_specs = out_specs + [pl.BlockSpec((1, 32, D_CONV), lambda b, i: (b, 0, 0)),
                                 pl.BlockSpec((1, D_ATTN, tm), tail_map),
                                 pl.BlockSpec((1, D_ATTN, tm), tail_map)]
        out_shape = out_shape + [jax.ShapeDtypeStruct((bsz, 32, D_CONV), F32),
                                 jax.ShapeDtypeStruct((bsz, D_ATTN, seq // 2), F32),
                                 jax.ShapeDtypeStruct((bsz, D_ATTN, seq // 2), F32)]
    return pl.pallas_call(
        functools.partial(_inproj_kernel, nt=nt, prompt=tails),
        grid=(bsz, nt),
        in_specs=in_specs,
        out_specs=out_specs,
        out_shape=out_shape,
        compiler_params=_params(("arbitrary", "arbitrary"), IN_PROJ_VMEM_LIMIT),
        name="in_proj",
    )(x, sc, sh, gmix, w_in, qg, kg, cos, sin, e)


def _conv_kernel(*refs, tc, rs, halo, has_prev, rc):
    if has_prev:
        g_ref, prev_ref, w_ref, b_ref, lg_ref, lb_ref, o_ref, buf = refs
    else:
        g_ref, w_ref, b_ref, lg_ref, lb_ref, o_ref, buf = refs
    i = pl.program_id(1)

    @pl.when(i == 0)
    def _():
        if has_prev:
            buf[0:halo, :] = prev_ref[0]
        else:
            buf[0:halo, :] = jnp.zeros((halo, D_CONV), F32)

    @pl.when(i > 0)
    def _():
        buf[0:halo, :] = buf[tc:tc + halo, :]

    buf[halo:halo + tc, :] = g_ref[0].astype(F32)
    base = halo - (CONV_A_WIDTH - 1) * rs
    bias = b_ref[...]
    lg = lg_ref[...]
    lb = lb_ref[...]
    for c in range(tc // rc):
        r0 = c * rc
        acc = jnp.broadcast_to(bias, (rc, D_CONV))
        if rs % SUBLANES == 0:
            for k in range(CONV_A_WIDTH):
                off = r0 + base + k * rs
                acc = acc + w_ref[k:k + 1, :] * buf[off:off + rc, :]
        else:
            n = rc + halo
            ext = buf[r0:r0 + n, :]
            for s in range(SUBLANES):
                rolled = ext if s == 0 else pltpu.roll(ext, n - s, 0)
                for k in range(CONV_A_WIDTH):
                    off = base + k * rs
                    if off % SUBLANES == s:
                        acc = acc + w_ref[k:k + 1, :] * rolled[off - s:off - s + rc, :]
        mu = jnp.mean(acc, axis=-1, keepdims=True)
        xc = acc - mu
        var = jnp.mean(xc * xc, axis=-1, keepdims=True)
        y = xc * lax.rsqrt(var + EPS) * lg + lb
        o_ref[0, r0:r0 + rc, :] = (y * _sigmoid(y)).astype(o_ref.dtype)


def _conv_mix(g, prev, w, b, lg, lb, *, tc, rs, out_dtype):
    bsz, seq, ch = g.shape
    has_prev = prev is not None
    halo = prev.shape[1] if has_prev else 32
    rc = min(tc, 32)
    in_specs = [pl.BlockSpec((1, tc, ch), lambda bb, i: (bb, i, 0))]
    args = [g]
    if has_prev:
        in_specs.append(pl.BlockSpec((1, halo, ch), lambda bb, i: (bb, 0, 0)))
        args.append(prev)
    in_specs += [pl.BlockSpec(w.shape, lambda bb, i: (0, 0))] + [pl.BlockSpec((1, ch), lambda bb, i: (0, 0))] * 3
    args += [w, b, lg, lb]
    return pl.pallas_call(
        functools.partial(_conv_kernel, tc=tc, rs=rs, halo=halo, has_prev=has_prev, rc=rc),
        grid=(bsz, seq // tc),
        in_specs=in_specs,
        out_specs=pl.BlockSpec((1, tc, ch), lambda bb, i: (bb, i, 0)),
        out_shape=jax.ShapeDtypeStruct((bsz, seq, ch), out_dtype),
        scratch_shapes=[pltpu.VMEM((halo + tc, ch), F32)],
        compiler_params=_params(("arbitrary", "arbitrary")),
        name="conv_mix",
    )(*args)


UNITS_PER_PHASE = 8


def _band_attn_kernel(q_ref, k_ref, v_ref, o_ref, lse_ref, *, lr, hb, ub):
    nblk = lr // BAND
    qi = lax.broadcasted_iota(jnp.int32, (BAND, 2 * BAND), 0)
    kj = lax.broadcasted_iota(jnp.int32, (BAND, 2 * BAND), 1)
    bias_band = jnp.where(kj >= qi, jnp.where(kj <= qi + BAND, 0.0, NEG_INF), NEG_INF).astype(F32)
    qi1 = lax.broadcasted_iota(jnp.int32, (BAND, BAND), 0)
    kj1 = lax.broadcasted_iota(jnp.int32, (BAND, BAND), 1)
    bias_first = jnp.where(kj1 <= qi1, 0.0, NEG_INF).astype(F32)

    def blocks(specs):
        units = [(q0, k0, nk, bias, hh) for (q0, k0, nk, bias) in specs for hh in range(hb)]
        for g in range(0, len(units), UNITS_PER_PHASE):
            phase_group(units[g:g + UNITS_PER_PHASE])

    def phase_group(units):
        scores = []
        for q0, k0, nk, bias, hh in units:
            lo = hh * HEAD_DIM
            qb = q_ref[0, pl.ds(q0, BAND), lo:lo + HEAD_DIM]
            kb = k_ref[0, pl.ds(k0, nk), lo:lo + HEAD_DIM]
            scores.append(lax.dot_general(qb, kb, (((1,), (1,)), ((), ())), preferred_element_type=F32) + bias)
        probs = []
        for s in scores:
            m = jnp.max(s, axis=-1, keepdims=True)
            p = jnp.exp(s - m)
            probs.append((p.astype(BF16), m, jnp.sum(p, axis=-1, keepdims=True)))
        for (q0, k0, nk, bias, hh), (p, m, l) in zip(units, probs):
            lo = hh * HEAD_DIM
            vb = v_ref[0, pl.ds(k0, nk), lo:lo + HEAD_DIM]
            o = jnp.dot(p, vb, preferred_element_type=F32) * (1.0 / l)
            o_ref[0, pl.ds(q0, BAND), lo:lo + HEAD_DIM] = o.astype(o_ref.dtype)
            lse_ref[0, 0, 0, pl.ds(q0, BAND), hh:hh + 1] = m + jnp.log(l)

    blocks([(0, 0, BAND, bias_first)])

    def banded(i):
        return (pl.multiple_of(i * BAND, BAND), pl.multiple_of((i - 1) * BAND, BAND), 2 * BAND, bias_band)

    groups = (nblk - 1) // ub

    def body(gi, carry):
        blocks([banded(1 + gi * ub + j) for j in range(ub)])
        return carry

    lax.fori_loop(0, groups, body, 0)
    rest = list(range(1 + groups * ub, nblk))
    if rest:
        blocks([banded(i) for i in rest])


def _band_attn(q, k, v, dil, width, ub):
    bsz, lr, cols = q.shape
    da = cols // dil
    seq = lr * dil
    hb = width // HEAD_DIM
    ncol = da // width
    spec = pl.BlockSpec((1, lr, width), lambda b, c: (b, 0, c))
    o, lse = pl.pallas_call(
        functools.partial(_band_attn_kernel, lr=lr, hb=hb, ub=ub),
        grid=(bsz, dil * ncol),
        in_specs=[spec, spec, spec],
        out_specs=[spec,
                   pl.BlockSpec((1, 1, 1, lr, hb), lambda b, c: (b, c // ncol, c % ncol, 0, 0))],
        out_shape=[jax.ShapeDtypeStruct((bsz, lr, cols), BF16),
                   jax.ShapeDtypeStruct((bsz, dil, ncol, lr, hb), F32)],
        compiler_params=_params(("parallel", "parallel")),
        name=f"band_attn_d{dil}",
    )(q, k, v)
    lse = lse.transpose(0, 3, 1, 2, 4).reshape(bsz, seq, N_HEADS)
    return o, lse


DEC_T = 4
HEAD_GROUP = 4
GROUP_COLS = HEAD_GROUP * HEAD_DIM
NEW_LANE0 = LANES - DEC_T


def _sample_mult_tables():
    t = np.arange(HEAD_GROUP * DEC_T) % DEC_T

    def table(dist):
        m = np.zeros(dist.shape, np.float32)
        for window, dil in DIL_PATTERNS:
            m += ((dist >= 0) & (dist <= window) & (dist % dil == 0)).astype(np.float32)
        return m

    win = table(WIN_MAX + t[:, None] - np.arange(WIN_MAX)[None, :])
    lane = np.arange(LANES)
    new = table(np.where(lane >= NEW_LANE0, t[:, None] - (lane[None, :] - NEW_LANE0), -1))
    row_head = np.arange(HEAD_GROUP * DEC_T) // DEC_T
    own = (row_head[:, None] == (np.arange(GROUP_COLS) // HEAD_DIM)[None, :]).astype(np.float32)
    return win, new, own


def _sattn_kernel(q_ref, kt_ref, vt_ref, kn_ref, vn_ref, mw_ref, mn_ref, own_ref, o_ref, okt_ref, ovt_ref):
    q = q_ref[...]
    kt = kt_ref[...]
    kn = kn_ref[...]
    mw = mw_ref[...]
    mn = mn_ref[...]
    s_win = jnp.where(mw > 0.0, jnp.dot(q, kt.astype(BF16), preferred_element_type=F32), NEG_INF)
    s_new = jnp.where(mn > 0.0, jnp.dot(q, kn.astype(BF16), preferred_element_type=F32), NEG_INF)
    m = jnp.maximum(jnp.max(s_win, axis=-1, keepdims=True), jnp.max(s_new, axis=-1, keepdims=True))
    p_win = mw * jnp.exp(s_win - m)
    p_new = mn * jnp.exp(s_new - m)
    den = jnp.sum(p_win, axis=-1, keepdims=True) + jnp.sum(p_new, axis=-1, keepdims=True)
    vt = vt_ref[...]
    vn = vn_ref[...]
    contract_lanes = (((1,), (1,)), ((), ()))
    o = (lax.dot_general(p_win.astype(BF16), vt.astype(BF16), contract_lanes, preferred_element_type=F32)
         + lax.dot_general(p_new.astype(BF16), vn.astype(BF16), contract_lanes, preferred_element_type=F32))
    o = o * (1.0 / den) * own_ref[...]
    o_ref[...] = functools.reduce(lambda a, c: a + c,
                                  [o[h * DEC_T:(h + 1) * DEC_T] for h in range(HEAD_GROUP)])

    lane = lax.broadcasted_iota(jnp.int32, kn.shape, 1)
    last = WIN_MAX - LANES
    for src, new, dst in ((kt, kn, okt_ref), (vt, vn, ovt_ref)):
        rolled = pltpu.roll(src, WIN_MAX - DEC_T, 1)
        dst[...] = rolled
        dst[:, last:] = jnp.where(lane >= NEW_LANE0, new, rolled[:, last:])


def _sample_attn(q_blk, kt, vt, kn, vn):
    bd = kt.shape[0]
    ng = N_HEADS // HEAD_GROUP
    win, new, own = (jnp.asarray(t) for t in _sample_mult_tables())
    win_spec = pl.BlockSpec((None, GROUP_COLS, WIN_MAX), lambda b, g: (b, g, 0))
    new_spec = pl.BlockSpec((None, GROUP_COLS, LANES), lambda b, g: (b, g, 0))
    const = lambda a: pl.BlockSpec(a.shape, lambda b, g: (0, 0))
    return pl.pallas_call(
        _sattn_kernel,
        grid=(bd, ng),
        in_specs=[pl.BlockSpec((None, None, HEAD_GROUP * DEC_T, GROUP_COLS), lambda b, g: (b, g, 0, 0)),
                  win_spec, win_spec, new_spec, new_spec, const(win), const(new), const(own)],
        out_specs=[pl.BlockSpec((None, DEC_T, GROUP_COLS), lambda b, g: (b, 0, g)), win_spec, win_spec],
        out_shape=[jax.ShapeDtypeStruct((bd, DEC_T, D_ATTN), F32),
                   jax.ShapeDtypeStruct(kt.shape, kt.dtype),
                   jax.ShapeDtypeStruct(vt.shape, vt.dtype)],
        compiler_params=_params(("parallel", "parallel")),
        name="sample_attn",
    )(q_blk, kt, vt, kn, vn, win, new, own)


def _outproj_kernel(*refs, n_pat):
    a_ref = refs[0]
    o_refs = refs[1:1 + n_pat]
    pos = 1 + n_pat
    lse_refs = ()
    if n_pat > 1:
        lse_refs = refs[pos:pos + n_pat]
        pos += n_pat
    x_ref, g1_ref, sc_ref, sh_ref, gn_ref, w_ref, e_ref, x1_ref, h2_ref = refs[pos:]

    if n_pat > 1:
        lses = [r[0] for r in lse_refs]
        mx = functools.reduce(jnp.maximum, lses)
        es = [jnp.exp(l - mx) for l in lses]
        inv = 1.0 / functools.reduce(lambda a, c: a + c, es)
        e16 = e_ref[...]
        o = None
        for ep, o_ref in zip(es, o_refs):
            wp = ep * inv
            hi = wp.astype(BF16)
            lo = (wp - hi.astype(F32)).astype(BF16)
            wb = (jnp.dot(hi, e16, preferred_element_type=F32)
                  + jnp.dot(lo, e16, preferred_element_type=F32))
            term = wb * o_ref[0].reshape(wb.shape).astype(F32)
            o = term if o is None else o + term
        ob = o.astype(BF16)
    else:
        ob = o_refs[0][0].astype(BF16)

    mix = (jnp.dot(a_ref[0].astype(BF16), w_ref[0:D_CONV, :], preferred_element_type=F32)
           + jnp.dot(ob, w_ref[D_CONV:, :], preferred_element_type=F32))
    x1 = x_ref[0] + g1_ref[0] * mix
    x1_ref[0] = x1
    ms = jnp.mean(x1 * x1, axis=-1, keepdims=True)
    h2 = x1 * lax.rsqrt(ms + EPS) * gn_ref[...]
    h2_ref[0] = (h2 * (1.0 + sc_ref[0]) + sh_ref[0]).astype(h2_ref.dtype)


def _out_proj(a_out, outs, lses, x, mod, gn, w_out, e16, *, tm):
    bsz, seq, d = x.shape
    n_pat = len(outs)
    lm = mod.shape[1]
    mod_rows = tm if lm == seq else 1

    def mod_map(piece):
        if lm == seq:
            return lambda b, i: (b, i, piece)
        return lambda b, i: (b, 0, piece)

    half_spec = pl.BlockSpec((1, tm, D_CONV), lambda b, i: (b, i, 0))
    full_spec = pl.BlockSpec((1, tm, d), lambda b, i: (b, i, 0))
    in_specs = [half_spec]
    for o in outs:
        dil = seq // o.shape[1]
        in_specs.append(pl.BlockSpec((1, tm // dil, dil * D_ATTN), lambda b, i: (b, i, 0)))
    args = [a_out, *outs]
    if n_pat > 1:
        in_specs += [pl.BlockSpec((1, tm, N_HEADS), lambda b, i: (b, i, 0))] * n_pat
        args += list(lses)
    in_specs += [full_spec,
                 pl.BlockSpec((1, mod_rows, d), mod_map(2)),
                 pl.BlockSpec((1, mod_rows, d), mod_map(4)),
                 pl.BlockSpec((1, mod_rows, d), mod_map(3)),
                 pl.BlockSpec((1, d), lambda b, i: (0, 0)),
                 pl.BlockSpec(w_out.shape, lambda b, i: (0, 0)),
                 pl.BlockSpec(e16.shape, lambda b, i: (0, 0))]
    args += [x, mod, mod, mod, gn, w_out, e16]
    return pl.pallas_call(
        functools.partial(_outproj_kernel, n_pat=n_pat),
        grid=(bsz, seq // tm),
        in_specs=in_specs,
        out_specs=[full_spec, full_spec],
        out_shape=[jax.ShapeDtypeStruct((bsz, seq, d), F32), jax.ShapeDtypeStruct((bsz, seq, d), BF16)],
        compiler_params=_params(("parallel", "parallel")),
        name="out_proj",
    )(*args)


def _ffn_kernel(*refs, tm, rs, halo, has_prev, nf):
    (h2_ref, x1_ref, g2_ref, wg_ref, wv_ref, wd_ref, cwg_ref, cwv_ref, cbg_ref, cbv_ref) = refs[:10]
    pos = 10
    prev_refs = (None, None)
    if has_prev:
        prev_refs = refs[pos:pos + 2]
        pos += 2
    y_ref, tg_ref, tv_ref, acc_ref, ug_buf, uv_buf, carry_ref = refs[pos:]
    m = pl.program_id(1)
    f = pl.program_id(2)
    h2 = h2_ref[0]

    def conv_half(w_ref, cw_ref, cb_ref, prev_ref, ubuf, t_ref, slot):
        u = jnp.dot(h2, w_ref[...], preferred_element_type=F32)

        @pl.when(m == 0)
        def _():
            if has_prev:
                ubuf[0:halo, :] = prev_ref[0]
            else:
                ubuf[0:halo, :] = jnp.zeros((halo, u.shape[1]), F32)

        @pl.when(m > 0)
        def _():
            ubuf[0:halo, :] = carry_ref[f, slot]

        ubuf[halo:halo + tm, :] = u
        last = u[tm - halo:, :]
        carry_ref[f, slot] = last
        t_ref[0] = last
        return (cb_ref[...] + cw_ref[0:1, :] * ubuf[halo - 2 * rs:halo - 2 * rs + tm, :]
                + cw_ref[1:2, :] * ubuf[halo - rs:halo - rs + tm, :] + cw_ref[2:3, :] * u)

    gate = conv_half(wg_ref, cwg_ref, cbg_ref, prev_refs[0], ug_buf, tg_ref, 0)
    val = conv_half(wv_ref, cwv_ref, cbv_ref, prev_refs[1], uv_buf, tv_ref, 1)
    act = (gate * _sigmoid(gate) * val).astype(BF16)
    contrib = jnp.dot(act, wd_ref[...], preferred_element_type=F32)

    @pl.when(f == 0)
    def _():
        acc_ref[...] = contrib

    @pl.when(f > 0)
    def _():
        acc_ref[...] += contrib

    @pl.when(f == nf - 1)
    def _():
        y_ref[0] = x1_ref[0] + g2_ref[0] * acc_ref[...]


def _conv_ffn(h2, x1, mod, w_up, w_down, cw, cb, prev, *, tm, tf, rs):
    bsz, seq, d = x1.shape
    nf = D_FF // tf
    has_prev = prev is not None
    halo = 2 * rs if has_prev else SUBLANES
    lm = mod.shape[1]
    mod_rows = tm if lm == seq else 1
    g2_map = (lambda b, m, f: (b, m, 5)) if lm == seq else (lambda b, m, f: (b, 0, 5))
    row = lambda b, m, f: (b, m, 0)
    gate_col = lambda b, m, f: (0, f)
    val_col = lambda b, m, f: (0, nf + f)
    in_specs = [pl.BlockSpec((1, tm, d), row), pl.BlockSpec((1, tm, d), row),
                pl.BlockSpec((1, mod_rows, d), g2_map),
                pl.BlockSpec((d, tf), gate_col), pl.BlockSpec((d, tf), val_col),
                pl.BlockSpec((tf, d), lambda b, m, f: (f, 0)),
                pl.BlockSpec((FFN_CONV_WIDTH, tf), gate_col), pl.BlockSpec((FFN_CONV_WIDTH, tf), val_col),
                pl.BlockSpec((1, tf), gate_col), pl.BlockSpec((1, tf), val_col)]
    args = [h2, x1, mod, w_up, w_up, w_down, cw, cw, cb, cb]
    if has_prev:
        in_specs += [pl.BlockSpec((1, halo, tf), lambda b, m, f: (b, 0, f)),
                     pl.BlockSpec((1, halo, tf), lambda b, m, f: (b, 0, nf + f))]
        args += [prev, prev]
    nm = seq // tm
    tail_spec = pl.BlockSpec((1, halo, tf), lambda b, m, f: (b, m, f))
    tail_shape = jax.ShapeDtypeStruct((bsz, nm * halo, D_FF), F32)
    y, tail_g, tail_v = pl.pallas_call(
        functools.partial(_ffn_kernel, tm=tm, rs=rs, halo=halo, has_prev=has_prev, nf=nf),
        grid=(bsz, nm, nf),
        in_specs=in_specs,
        out_specs=[pl.BlockSpec((1, tm, d), row), tail_spec, tail_spec],
        out_shape=[jax.ShapeDtypeStruct((bsz, seq, d), F32), tail_shape, tail_shape],
        scratch_shapes=[pltpu.VMEM((tm, d), F32),
                        pltpu.VMEM((halo + tm, tf), F32), pltpu.VMEM((halo + tm, tf), F32),
                        pltpu.VMEM((nf, 2, halo, tf), F32)],
        compiler_params=_params(("arbitrary", "arbitrary", "arbitrary")),
        name="conv_ffn",
    )(*args)
    last = (nm - 1) * halo
    return y, jnp.concatenate([tail_g[:, last:], tail_v[:, last:]], axis=-1)


def _ffn_pipe_kernel(h2_ref, x1_ref, g2_ref, wg_ref, wv_ref, wd_ref, cwg_ref, cwv_ref, cbg_ref, cbv_ref,
                     y_ref, tg_ref, tv_ref, acc_ref, ua_ref, ub_ref, carry_ref, *, tm, nf):
    m = pl.program_id(1)
    f = pl.program_id(2)
    fb = jnp.maximum(f - 1, 0)

    @pl.when(f == 0)
    def _():
        ub_ref[...] = jnp.zeros(ub_ref.shape, F32)

        @pl.when(m == 0)
        def _():
            carry_ref[...] = jnp.zeros(carry_ref.shape, F32)

    def step(u_rd, u_wr):
        h2 = h2_ref[0]
        u_wr[0] = jnp.dot(h2, wg_ref[...], preferred_element_type=F32)
        u_wr[1] = jnp.dot(h2, wv_ref[...], preferred_element_type=F32)

        def conv_half(slot, cw_ref, cb_ref, t_ref):
            u = u_rd[slot]
            kept = carry_ref[fb, slot]
            prev = jnp.where(m == 0, 0.0, kept)
            last = u[tm - SUBLANES:, :]
            carry_ref[fb, slot] = jnp.where(f == 0, kept, last)
            t_ref[0] = last
            ext = jnp.concatenate([prev, u], axis=0)
            u1 = pltpu.roll(ext, 1, 0)[SUBLANES:]
            u2 = pltpu.roll(ext, 2, 0)[SUBLANES:]
            return cb_ref[...] + cw_ref[0:1, :] * u2 + cw_ref[1:2, :] * u1 + cw_ref[2:3, :] * u

        gate = conv_half(0, cwg_ref, cbg_ref, tg_ref)
        val = conv_half(1, cwv_ref, cbv_ref, tv_ref)
        act = (gate * _sigmoid(gate) * val).astype(BF16)
        contrib = jnp.dot(act, wd_ref[...], preferred_element_type=F32)
        acc_ref[...] = jnp.where(f <= 1, 0.0, acc_ref[...]) + contrib

    @pl.when(lax.rem(f, 2) == 0)
    def _():
        step(ub_ref, ua_ref)

    @pl.when(lax.rem(f, 2) == 1)
    def _():
        step(ua_ref, ub_ref)

    @pl.when(f == nf)
    def _():
        y_ref[0] = x1_ref[0] + g2_ref[0] * acc_ref[...]


def _conv_ffn_prompt(h2, x1, mod, w_up, w_down, cw, cb, *, tm, tf):
    bsz, seq, d = x1.shape
    nf = D_FF // tf
    nm = seq // tm
    row = lambda b, m, f: (b, m, 0)
    up = lambda f: jnp.minimum(f, nf - 1)
    down = lambda f: jnp.maximum(f - 1, 0)
    in_specs = [pl.BlockSpec((1, tm, d), row), pl.BlockSpec((1, tm, d), row),
                pl.BlockSpec((1, 1, d), lambda b, m, f: (b, 0, 5)),
                pl.BlockSpec((d, tf), lambda b, m, f: (0, up(f))),
                pl.BlockSpec((d, tf), lambda b, m, f: (0, nf + up(f))),
                pl.BlockSpec((tf, d), lambda b, m, f: (down(f), 0)),
                pl.BlockSpec((FFN_CONV_WIDTH, tf), lambda b, m, f: (0, down(f))),
                pl.BlockSpec((FFN_CONV_WIDTH, tf), lambda b, m, f: (0, nf + down(f))),
                pl.BlockSpec((1, tf), lambda b, m, f: (0, down(f))),
                pl.BlockSpec((1, tf), lambda b, m, f: (0, nf + down(f)))]
    tail_spec = pl.BlockSpec((1, SUBLANES, tf), lambda b, m, f: (b, m, down(f)))
    tail_shape = jax.ShapeDtypeStruct((bsz, nm * SUBLANES, D_FF), F32)
    y, tail_g, tail_v = pl.pallas_call(
        functools.partial(_ffn_pipe_kernel, tm=tm, nf=nf),
        grid=(bsz, nm, nf + 1),
        in_specs=in_specs,
        out_specs=[pl.BlockSpec((1, tm, d), row), tail_spec, tail_spec],
        out_shape=[jax.ShapeDtypeStruct((bsz, seq, d), F32), tail_shape, tail_shape],
        scratch_shapes=[pltpu.VMEM((tm, d), F32),
                        pltpu.VMEM((2, tm, tf), F32), pltpu.VMEM((2, tm, tf), F32),
                        pltpu.VMEM((nf, 2, SUBLANES, tf), F32)],
        compiler_params=_params(("arbitrary", "arbitrary", "arbitrary")),
        name="conv_ffn_pipe",
    )(h2, x1, mod, w_up, w_up, w_down, cw, cw, cb, cb)
    last = (nm - 1) * SUBLANES
    return y, jnp.concatenate([tail_g[:, last:], tail_v[:, last:]], axis=-1)


def _rope_tables(pos):
    half = HEAD_DIM // 2
    inv = ROPE_THETA ** (-jnp.arange(half, dtype=F32) / half)
    ang = pos.astype(F32)[:, None] * inv[None, :]
    cos = jnp.cos(ang)
    sin = jnp.sin(ang)
    cos_l = jnp.concatenate([cos, cos, cos, cos], axis=-1)
    sin_l = jnp.concatenate([-sin, sin, -sin, sin], axis=-1)
    return cos_l, sin_l


def _head_mean_matrix():
    h = np.arange(D_ATTN) // HEAD_DIM
    return jnp.asarray((h[:, None] == h[None, :]).astype(np.float32) / HEAD_DIM, dtype=BF16)


def _head_expand_matrix():
    h = np.arange(D_ATTN) // HEAD_DIM
    return jnp.asarray((np.arange(N_HEADS)[:, None] == h[None, :]).astype(np.float32), dtype=BF16)


def kernel(x_prompt, x_sample, cache_win_k, cache_win_v, state_conv_a, state_ffn_conv, c_prompt, c_sample,
           norm_mix_g, norm_ffn_g, w_ada, b_ada, w_in, conv_a_w, conv_a_b, ln_a_g, ln_a_b, q_norm_g, k_norm_g,
           w_out, w_up, ffn_conv_w, ffn_conv_b, w_down):
    bsz, seq, d = x_prompt.shape
    bd, dt, _ = x_sample.shape
    assert w_ada.shape[0] == 1 and dt == DEC_T and cache_win_k.shape[2] == WIN_MAX and seq == 2 * WIN_MAX

    w_in_b = w_in[0].astype(BF16)
    w_out_b = w_out[0].astype(BF16)
    w_up_b = w_up[0].astype(BF16)
    w_down_b = w_down[0].astype(BF16)
    e_mean = _head_mean_matrix()
    e16 = _head_expand_matrix()
    qg = jnp.tile(q_norm_g[0], N_HEADS)[None, :]
    kg = jnp.tile(k_norm_g[0], N_HEADS)[None, :]

    c_all = jnp.concatenate([c_prompt, c_sample, jnp.zeros((16 - bsz - bd, d), F32)], axis=0)
    mod = _ada(c_all, w_ada[0], b_ada)
    mod_p = mod[:bsz][:, None, :]
    mod_s = jnp.tile(mod[bsz:bsz + bd], (dt, 1))[None]

    cos_p, sin_p = _rope_tables(jnp.arange(seq))
    g_p, *qkv_views, g_tail, k_tail, v_tail = _in_proj(
        x_prompt, mod_p, mod_p, norm_mix_g, w_in_b, qg, kg, cos_p, sin_p, e_mean,
        tm=256, out_dtype=BF16, tails=True)
    a_p = _conv_mix(g_p, None, conv_a_w[0], conv_a_b, ln_a_g, ln_a_b, tc=256, rs=1, out_dtype=BF16)
    outs, lses = [], []
    for n, ((window, dil), width, ub) in enumerate(zip(DIL_PATTERNS, (256, 512, 512), (2, 1, 1))):
        o, lse = _band_attn(*qkv_views[3 * n:3 * n + 3], dil, width, ub)
        outs.append(o)
        lses.append(lse)
    x1_p, h2_p = _out_proj(a_p, outs, lses, x_prompt, mod_p, norm_ffn_g, w_out_b, e16, tm=256)
    y_p, ftail_p = _conv_ffn_prompt(h2_p, x1_p, mod_p, w_up_b, w_down_b, ffn_conv_w[0], ffn_conv_b,
                                    tm=512, tf=512)

    rows = dt * bd
    to_tm = lambda a: a.transpose(1, 0, 2).reshape(1, a.shape[0] * a.shape[1], a.shape[2])
    xs = to_tm(x_sample)
    cos_s, sin_s = _rope_tables(PAST_LEN + jnp.arange(rows) // bd)
    g_s, q_s, k_s, v_s = _in_proj(xs, mod_s, mod_s, norm_mix_g, w_in_b, qg, kg, cos_s, sin_s, e_mean,
                                  tm=rows, out_dtype=F32, tails=False)
    a_s = _conv_mix(g_s, to_tm(state_conv_a[0]), conv_a_w[0], conv_a_b, ln_a_g, ln_a_b,
                    tc=rows, rs=bd, out_dtype=BF16)
    heads = lambda a: a.reshape(dt, bd, N_HEADS, HEAD_DIM)
    ng = N_HEADS // HEAD_GROUP
    qg5 = heads(q_s).reshape(dt, bd, ng, HEAD_GROUP, HEAD_DIM).transpose(1, 2, 3, 0, 4)
    q_blk = (qg5[:, :, :, :, None, :] * jnp.eye(HEAD_GROUP, dtype=F32)[None, None, :, None, :, None])
    q_blk = q_blk.reshape(bd, ng, HEAD_GROUP * dt, GROUP_COLS).astype(BF16)
    to_pos_minor = lambda c: c[0].transpose(0, 2, 3, 1).reshape(bd, D_ATTN, WIN_MAX)
    new_tile = lambda a: jnp.pad(heads(a).transpose(1, 2, 3, 0).reshape(bd, D_ATTN, dt),
                                 ((0, 0), (0, 0), (NEW_LANE0, 0)))
    o_s, win_kt, win_vt = _sample_attn(q_blk, to_pos_minor(cache_win_k), to_pos_minor(cache_win_v),
                                       new_tile(k_s), new_tile(v_s))
    o_s = o_s.transpose(1, 0, 2).reshape(1, rows, D_ATTN)
    from_pos_minor = lambda a: a.reshape(a.shape[0], N_HEADS, HEAD_DIM, a.shape[-1]).transpose(0, 3, 1, 2)[None]
    win_k_s, win_v_s = from_pos_minor(win_kt), from_pos_minor(win_vt)
    x1_s, h2_s = _out_proj(a_s, [o_s], None, xs, mod_s, norm_ffn_g, w_out_b, e16, tm=rows)
    y_s, ftail_s = _conv_ffn(h2_s, x1_s, mod_s, w_up_b, w_down_b, ffn_conv_w[0], ffn_conv_b,
                             to_tm(state_ffn_conv[0]), tm=rows, tf=512, rs=bd)

    from_tm = lambda a, t: a.reshape(t, bd, a.shape[-1]).transpose(1, 0, 2)
    conv_a_s = jnp.concatenate([state_conv_a[0], from_tm(g_s[0], dt)], axis=1)[:, dt:]
    return (y_p, from_tm(y_s[0], dt),
            from_pos_minor(k_tail), from_pos_minor(v_tail),
            g_tail[:, 32 - (CONV_A_WIDTH - 1):][None],
            ftail_p[:, SUBLANES - (FFN_CONV_WIDTH - 1):][None],
            win_k_s, win_v_s,
            conv_a_s[None],
            from_tm(ftail_s[0], FFN_CONV_WIDTH - 1)[None])
```

```python
import functools

import numpy as np
import jax
import jax.numpy as jnp
from jax import lax
from jax.experimental import pallas as pl
from jax.experimental.pallas import tpu as pltpu

F32 = jnp.float32
BF16 = jnp.bfloat16

D_MODEL = 2048
HEAD_DIM = 64
D_CONV = D_MODEL // 2
N_HEADS = (D_MODEL // 2) // HEAD_DIM
D_ATTN = N_HEADS * HEAD_DIM
CONV_A_WIDTH = 31
DIL_PATTERNS = ((128, 1), (512, 4), (2048, 16))
BAND = 128
WIN_MAX = 2048
ROPE_THETA = 10000.0
D_FF = 5632
FFN_CONV_WIDTH = 3
EPS = 1e-6
NEG_INF = -1e30
PAST_LEN = 16384

LANES = 128
SUBLANES = 8
VMEM_LIMIT = 56 * 1024 * 1024
IN_PROJ_VMEM_LIMIT = 60 * 1024 * 1024


def _sigmoid(x):
    return 1.0 / (1.0 + jnp.exp(-x))


def _params(sem, vmem=VMEM_LIMIT, flags=None):
    return pltpu.CompilerParams(dimension_semantics=sem, vmem_limit_bytes=vmem, flags=flags)


def _ada_kernel(c_ref, w_ref, b_ref, o_ref):
    c = c_ref[...]
    s = (c * _sigmoid(c)).astype(BF16)
    o_ref[...] = jnp.dot(s, w_ref[...].astype(BF16), preferred_element_type=F32) + b_ref[...]


def _ada(c_all, w_ada, b_ada):
    rows, d = c_all.shape
    n = w_ada.shape[1]
    tn = 1024
    return pl.pallas_call(
        _ada_kernel,
        grid=(n // tn,),
        in_specs=[pl.BlockSpec((rows, d), lambda j: (0, 0)),
                  pl.BlockSpec((d, tn), lambda j: (0, j)),
                  pl.BlockSpec((1, tn), lambda j: (0, j))],
        out_specs=pl.BlockSpec((rows, tn), lambda j: (0, j)),
        out_shape=jax.ShapeDtypeStruct((rows, n), F32),
        compiler_params=_params(("parallel",)),
        name="ada_mod",
    )(c_all, w_ada, b_ada)


def _head_norm_rope(x, gain, cos, sin, e):
    ms = jnp.dot((x * x).astype(BF16), e, preferred_element_type=F32)
    xn = x * lax.rsqrt(ms + EPS) * gain
    lane = lax.broadcasted_iota(jnp.int32, (1, LANES), 1)
    first_half = jnp.bitwise_and(lane, HEAD_DIM - 1) < (HEAD_DIM // 2)
    out = []
    for c in range(D_ATTN // LANES):
        xc = xn[:, c * LANES:(c + 1) * LANES]
        partner = jnp.where(first_half, pltpu.roll(xc, LANES - HEAD_DIM // 2, 1),
                            pltpu.roll(xc, HEAD_DIM // 2, 1))
        out.append(xc * cos + partner * sin)
    return out


def _inproj_kernel(x_ref, sc_ref, sh_ref, gmix_ref, w_ref, qg_ref, kg_ref, cos_ref, sin_ref, e_ref,
                   g_ref, q_ref, k_ref, v_ref, *extra_refs, nt, prompt):
    x = x_ref[0]
    ms = jnp.mean(x * x, axis=-1, keepdims=True)
    h = x * lax.rsqrt(ms + EPS) * gmix_ref[...]
    hb = (h * (1.0 + sc_ref[0]) + sh_ref[0]).astype(BF16)

    def proj(j):
        return jnp.dot(hb, w_ref[:, j * D_CONV:(j + 1) * D_CONV], preferred_element_type=F32)

    g = proj(0) * _sigmoid(proj(1))
    g_ref[0] = g.astype(g_ref.dtype)
    cos = cos_ref[...]
    sin = sin_ref[...]
    e = e_ref[...]
    q = jnp.concatenate(_head_norm_rope(proj(2), qg_ref[...], cos, sin, e), axis=1) * (HEAD_DIM ** -0.5)
    k = jnp.concatenate(_head_norm_rope(proj(3), kg_ref[...], cos, sin, e), axis=1)
    v = proj(4)
    qkv = [a.astype(q_ref.dtype) for a in (q, k, v)]
    for ref, a in zip((q_ref, k_ref, v_ref), qkv):
        ref[0] = a

    if prompt:
        view_refs = extra_refs[:6]
        gt_ref, kt_ref, vt_ref = extra_refs[6:]
        tm = x.shape[0]
        for n, (_, dil) in enumerate(DIL_PATTERNS[1:]):
            for ref, a in zip(view_refs[3 * n:3 * n + 3], qkv):
                ref[0] = a.reshape(tm // dil, dil * D_ATTN)
        i = pl.program_id(1)
        rows = gt_ref.shape[1]

        @pl.when(i == nt - 1)
        def _():
            gt_ref[0] = g[g.shape[0] - rows:, :]

        @pl.when(i >= nt // 2)
        def _():
            kt_ref[0] = k.T
            vt_ref[0] = v.T


def _in_proj(x, sc, sh, gmix, w_in, qg, kg, cos, sin, e, *, tm, out_dtype, tails):
    bsz, seq, d = x.shape
    nt = seq // tm
    lm = sc.shape[1]
    mod_rows = tm if lm == seq else 1

    def mod_map(piece):
        if lm == seq:
            return lambda b, i: (b, i, piece)
        return lambda b, i: (b, 0, piece)

    row_spec = pl.BlockSpec((1, tm, D_CONV), lambda b, i: (b, i, 0))
    in_specs = [
        pl.BlockSpec((1, tm, d), lambda b, i: (b, i, 0)),
        pl.BlockSpec((1, mod_rows, d), mod_map(1)),
        pl.BlockSpec((1, mod_rows, d), mod_map(0)),
        pl.BlockSpec((1, d), lambda b, i: (0, 0)),
        pl.BlockSpec(w_in.shape, lambda b, i: (0, 0)),
        pl.BlockSpec((1, D_ATTN), lambda b, i: (0, 0)),
        pl.BlockSpec((1, D_ATTN), lambda b, i: (0, 0)),
        pl.BlockSpec((tm, LANES), lambda b, i: (i, 0)),
        pl.BlockSpec((tm, LANES), lambda b, i: (i, 0)),
        pl.BlockSpec(e.shape, lambda b, i: (0, 0)),
    ]
    out_specs = [row_spec] * 4
    out_shape = [jax.ShapeDtypeStruct((bsz, seq, D_CONV), out_dtype)] * 4
    if tails:
        half = nt // 2
        tail_map = lambda b, i: (b, 0, jnp.maximum(i - half, 0))
        for _, dil in DIL_PATTERNS[1:]:
            out_specs = out_specs + [pl.BlockSpec((1, tm // dil, dil * D_ATTN), lambda b, i: (b, i, 0))] * 3
            out_shape = out_shape + [jax.ShapeDtypeStruct((bsz, seq // dil, dil * D_ATTN), out_dtype)] * 3
        out_specs = out_specs + [pl.BlockSpec((1, 32, D_CONV), lambda b, i: (b, 0, 0)),
                                 pl.BlockSpec((1, D_ATTN, tm), tail_map),
                                 pl.BlockSpec((1, D_ATTN, tm), tail_map)]
        out_shape = out_shape + [jax.ShapeDtypeStruct((bsz, 32, D_CONV), F32),
                                 jax.ShapeDtypeStruct((bsz, D_ATTN, seq // 2), F32),
                                 jax.ShapeDtypeStruct((bsz, D_ATTN, seq // 2), F32)]
    return pl.pallas_call(
        functools.partial(_inproj_kernel, nt=nt, prompt=tails),
        grid=(bsz, nt),
        in_specs=in_specs,
        out_specs=out_specs,
        out_shape=out_shape,
        compiler_params=_params(("arbitrary", "arbitrary"), IN_PROJ_VMEM_LIMIT),
        name="in_proj",
    )(x, sc, sh, gmix, w_in, qg, kg, cos, sin, e)


def _conv_kernel(*refs, tc, rs, halo, has_prev, rc):
    if has_prev:
        g_ref, prev_ref, w_ref, b_ref, lg_ref, lb_ref, o_ref, buf = refs
    else:
        g_ref, w_ref, b_ref, lg_ref, lb_ref, o_ref, buf = refs
    i = pl.program_id(1)

    @pl.when(i == 0)
    def _():
        if has_prev:
            buf[0:halo, :] = prev_ref[0]
        else:
            buf[0:halo, :] = jnp.zeros((halo, D_CONV), F32)

    @pl.when(i > 0)
    def _():
        buf[0:halo, :] = buf[tc:tc + halo, :]

    buf[halo:halo + tc, :] = g_ref[0].astype(F32)
    base = halo - (CONV_A_WIDTH - 1) * rs
    bias = b_ref[...]
    lg = lg_ref[...]
    lb = lb_ref[...]
    for c in range(tc // rc):
        r0 = c * rc
        acc = jnp.broadcast_to(bias, (rc, D_CONV))
        if rs % SUBLANES == 0:
            for k in range(CONV_A_WIDTH):
                off = r0 + base + k * rs
                acc = acc + w_ref[k:k + 1, :] * buf[off:off + rc, :]
        else:
            n = rc + halo
            ext = buf[r0:r0 + n, :]
            for s in range(SUBLANES):
                rolled = ext if s == 0 else pltpu.roll(ext, n - s, 0)
                for k in range(CONV_A_WIDTH):
                    off = base + k * rs
                    if off % SUBLANES == s:
                        acc = acc + w_ref[k:k + 1, :] * rolled[off - s:off - s + rc, :]
        mu = jnp.mean(acc, axis=-1, keepdims=True)
        xc = acc - mu
        var = jnp.mean(xc * xc, axis=-1, keepdims=True)
        y = xc * lax.rsqrt(var + EPS) * lg + lb
        o_ref[0, r0:r0 + rc, :] = (y * _sigmoid(y)).astype(o_ref.dtype)


def _conv_mix(g, prev, w, b, lg, lb, *, tc, rs, out_dtype):
    bsz, seq, ch = g.shape
    has_prev = prev is not None
    halo = prev.shape[1] if has_prev else 32
    rc = min(tc, 32)
    in_specs = [pl.BlockSpec((1, tc, ch), lambda bb, i: (bb, i, 0))]
    args = [g]
    if has_prev:
        in_specs.append(pl.BlockSpec((1, halo, ch), lambda bb, i: (bb, 0, 0)))
        args.append(prev)
    in_specs += [pl.BlockSpec(w.shape, lambda bb, i: (0, 0))] + [pl.BlockSpec((1, ch), lambda bb, i: (0, 0))] * 3
    args += [w, b, lg, lb]
    return pl.pallas_call(
        functools.partial(_conv_kernel, tc=tc, rs=rs, halo=halo, has_prev=has_prev, rc=rc),
        grid=(bsz, seq // tc),
        in_specs=in_specs,
        out_specs=pl.BlockSpec((1, tc, ch), lambda bb, i: (bb, i, 0)),
        out_shape=jax.ShapeDtypeStruct((bsz, seq, ch), out_dtype),
        scratch_shapes=[pltpu.VMEM((halo + tc, ch), F32)],
        compiler_params=_params(("arbitrary", "arbitrary")),
        name="conv_mix",
    )(*args)


PAIRS_PER_PHASE = 4


def _band_attn_kernel(q_ref, k_ref, v_ref, o_ref, lse_ref, *, lr, hb, ub):
    nblk = lr // BAND
    qi = lax.broadcasted_iota(jnp.int32, (BAND, 2 * BAND), 0)
    kj = lax.broadcasted_iota(jnp.int32, (BAND, 2 * BAND), 1)
    bias_band = jnp.where(kj >= qi, jnp.where(kj <= qi + BAND, 0.0, NEG_INF), NEG_INF).astype(F32)
    qi1 = lax.broadcasted_iota(jnp.int32, (BAND, BAND), 0)
    kj1 = lax.broadcasted_iota(jnp.int32, (BAND, BAND), 1)
    bias_first = jnp.where(kj1 <= qi1, 0.0, NEG_INF).astype(F32)

    lane = lax.broadcasted_iota(jnp.int32, (1, LANES), 1)
    even_lanes = lane < HEAD_DIM

    def blocks(specs):
        units = [(q0, k0, nk, bias, pr) for (q0, k0, nk, bias) in specs for pr in range(hb // 2)]
        for g in range(0, len(units), PAIRS_PER_PHASE):
            phase_group(units[g:g + PAIRS_PER_PHASE])

    def phase_group(units):
        scores = []
        for q0, k0, nk, bias, pr in units:
            cols = slice(pr * LANES, (pr + 1) * LANES)
            qp = q_ref[0, pl.ds(q0, BAND), cols]
            kp = k_ref[0, pl.ds(k0, nk), cols]
            q2 = jnp.concatenate([jnp.where(even_lanes, qp, 0), jnp.where(even_lanes, 0, qp)], axis=0)
            s = lax.dot_general(q2, kp, (((1,), (1,)), ((), ())), preferred_element_type=F32)
            scores.append(s + jnp.concatenate([bias, bias], axis=0))
        probs = []
        for s in scores:
            m = jnp.max(s, axis=-1, keepdims=True)
            probs.append((jnp.exp(s - m).astype(BF16), m))
        for (q0, k0, nk, bias, pr), (p, m) in zip(units, probs):
            cols = slice(pr * LANES, (pr + 1) * LANES)
            vp = v_ref[0, pl.ds(k0, nk), cols]
            one = jnp.ones((), vp.dtype)
            r_even = jnp.dot(p[:BAND], jnp.where(even_lanes, vp, one), preferred_element_type=F32)
            r_odd = jnp.dot(p[BAND:], jnp.where(even_lanes, one, vp), preferred_element_type=F32)
            num = jnp.where(even_lanes, r_even, r_odd)
            den = pltpu.roll(jnp.where(even_lanes, r_odd, r_even), HEAD_DIM, 1)
            o_ref[0, pl.ds(q0, BAND), cols] = (num * (1.0 / den)).astype(o_ref.dtype)
            log_den = jnp.log(den)
            lse_ref[0, 0, 0, pl.ds(q0, BAND), 2 * pr:2 * pr + 1] = m[:BAND] + log_den[:, 0:1]
            lse_ref[0, 0, 0, pl.ds(q0, BAND), 2 * pr + 1:2 * pr + 2] = m[BAND:] + log_den[:, HEAD_DIM:HEAD_DIM + 1]

    blocks([(0, 0, BAND, bias_first)])

    def banded(i):
        return (pl.multiple_of(i * BAND, BAND), pl.multiple_of((i - 1) * BAND, BAND), 2 * BAND, bias_band)

    groups = (nblk - 1) // ub

    def body(gi, carry):
        blocks([banded(1 + gi * ub + j) for j in range(ub)])
        return carry

    lax.fori_loop(0, groups, body, 0)
    rest = list(range(1 + groups * ub, nblk))
    if rest:
        blocks([banded(i) for i in rest])


def _band_attn(q, k, v, dil, width, ub):
    bsz, lr, cols = q.shape
    da = cols // dil
    seq = lr * dil
    hb = width // HEAD_DIM
    ncol = da // width
    spec = pl.BlockSpec((1, lr, width), lambda b, c: (b, 0, c))
    o, lse = pl.pallas_call(
        functools.partial(_band_attn_kernel, lr=lr, hb=hb, ub=ub),
        grid=(bsz, dil * ncol),
        in_specs=[spec, spec, spec],
        out_specs=[spec,
                   pl.BlockSpec((1, 1, 1, lr, hb), lambda b, c: (b, c // ncol, c % ncol, 0, 0))],
        out_shape=[jax.ShapeDtypeStruct((bsz, lr, cols), BF16),
                   jax.ShapeDtypeStruct((bsz, dil, ncol, lr, hb), F32)],
        compiler_params=_params(("parallel", "parallel")),
        name=f"band_attn_d{dil}",
    )(q, k, v)
    lse = lse.transpose(0, 3, 1, 2, 4).reshape(bsz, seq, N_HEADS)
    return o, lse


DEC_T = 4
HEAD_GROUP = 4
GROUP_COLS = HEAD_GROUP * HEAD_DIM
NEW_LANE0 = LANES - DEC_T


def _sample_mult_tables():
    t = np.arange(HEAD_GROUP * DEC_T) % DEC_T

    def table(dist):
        m = np.zeros(dist.shape, np.float32)
        for window, dil in DIL_PATTERNS:
            m += ((dist >= 0) & (dist <= window) & (dist % dil == 0)).astype(np.float32)
        return m

    win = table(WIN_MAX + t[:, None] - np.arange(WIN_MAX)[None, :])
    lane = np.arange(LANES)
    new = table(np.where(lane >= NEW_LANE0, t[:, None] - (lane[None, :] - NEW_LANE0), -1))
    row_head = np.arange(HEAD_GROUP * DEC_T) // DEC_T
    own = (row_head[:, None] == (np.arange(GROUP_COLS) // HEAD_DIM)[None, :]).astype(np.float32)
    return win, new, own


def _sattn_kernel(q_ref, kt_ref, vt_ref, kn_ref, vn_ref, mw_ref, mn_ref, own_ref, o_ref, okt_ref, ovt_ref):
    q = q_ref[...]
    kt = kt_ref[...]
    kn = kn_ref[...]
    mw = mw_ref[...]
    mn = mn_ref[...]
    s_win = jnp.where(mw > 0.0, jnp.dot(q, kt.astype(BF16), preferred_element_type=F32), NEG_INF)
    s_new = jnp.where(mn > 0.0, jnp.dot(q, kn.astype(BF16), preferred_element_type=F32), NEG_INF)
    m = jnp.maximum(jnp.max(s_win, axis=-1, keepdims=True), jnp.max(s_new, axis=-1, keepdims=True))
    p_win = mw * jnp.exp(s_win - m)
    p_new = mn * jnp.exp(s_new - m)
    den = jnp.sum(p_win, axis=-1, keepdims=True) + jnp.sum(p_new, axis=-1, keepdims=True)
    vt = vt_ref[...]
    vn = vn_ref[...]
    contract_lanes = (((1,), (1,)), ((), ()))
    o = (lax.dot_general(p_win.astype(BF16), vt.astype(BF16), contract_lanes, preferred_element_type=F32)
         + lax.dot_general(p_new.astype(BF16), vn.astype(BF16), contract_lanes, preferred_element_type=F32))
    o = o * (1.0 / den) * own_ref[...]
    o_ref[...] = functools.reduce(lambda a, c: a + c,
                                  [o[h * DEC_T:(h + 1) * DEC_T] for h in range(HEAD_GROUP)])

    lane = lax.broadcasted_iota(jnp.int32, kn.shape, 1)
    last = WIN_MAX - LANES
    for src, new, dst in ((kt, kn, okt_ref), (vt, vn, ovt_ref)):
        rolled = pltpu.roll(src, WIN_MAX - DEC_T, 1)
        dst[...] = rolled
        dst[:, last:] = jnp.where(lane >= NEW_LANE0, new, rolled[:, last:])


def _sample_attn(q_blk, kt, vt, kn, vn):
    bd = kt.shape[0]
    ng = N_HEADS // HEAD_GROUP
    win, new, own = (jnp.asarray(t) for t in _sample_mult_tables())
    win_spec = pl.BlockSpec((None, GROUP_COLS, WIN_MAX), lambda b, g: (b, g, 0))
    new_spec = pl.BlockSpec((None, GROUP_COLS, LANES), lambda b, g: (b, g, 0))
    const = lambda a: pl.BlockSpec(a.shape, lambda b, g: (0, 0))
    return pl.pallas_call(
        _sattn_kernel,
        grid=(bd, ng),
        in_specs=[pl.BlockSpec((None, None, HEAD_GROUP * DEC_T, GROUP_COLS), lambda b, g: (b, g, 0, 0)),
                  win_spec, win_spec, new_spec, new_spec, const(win), const(new), const(own)],
        out_specs=[pl.BlockSpec((None, DEC_T, GROUP_COLS), lambda b, g: (b, 0, g)), win_spec, win_spec],
        out_shape=[jax.ShapeDtypeStruct((bd, DEC_T, D_ATTN), F32),
                   jax.ShapeDtypeStruct(kt.shape, kt.dtype),
                   jax.ShapeDtypeStruct(vt.shape, vt.dtype)],
        compiler_params=_params(("parallel", "parallel")),
        name="sample_attn",
    )(q_blk, kt, vt, kn, vn, win, new, own)


def _outproj_kernel(*refs, n_pat):
    a_ref = refs[0]
    o_refs = refs[1:1 + n_pat]
    pos = 1 + n_pat
    lse_refs = ()
    if n_pat > 1:
        lse_refs = refs[pos:pos + n_pat]
        pos += n_pat
    x_ref, g1_ref, sc_ref, sh_ref, gn_ref, w_ref, e_ref, x1_ref, h2_ref = refs[pos:]

    if n_pat > 1:
        lses = [r[0] for r in lse_refs]
        mx = functools.reduce(jnp.maximum, lses)
        es = [jnp.exp(l - mx) for l in lses]
        inv = 1.0 / functools.reduce(lambda a, c: a + c, es)
        e16 = e_ref[...]
        o = None
        for ep, o_ref in zip(es, o_refs):
            wp = ep * inv
            hi = wp.astype(BF16)
            lo = (wp - hi.astype(F32)).astype(BF16)
            wb = (jnp.dot(hi, e16, preferred_element_type=F32)
                  + jnp.dot(lo, e16, preferred_element_type=F32))
            term = wb * o_ref[0].reshape(wb.shape).astype(F32)
            o = term if o is None else o + term
        ob = o.astype(BF16)
    else:
        ob = o_refs[0][0].astype(BF16)

    mix = (jnp.dot(a_ref[0].astype(BF16), w_ref[0:D_CONV, :], preferred_element_type=F32)
           + jnp.dot(ob, w_ref[D_CONV:, :], preferred_element_type=F32))
    x1 = x_ref[0] + g1_ref[0] * mix
    x1_ref[0] = x1
    ms = jnp.mean(x1 * x1, axis=-1, keepdims=True)
    h2 = x1 * lax.rsqrt(ms + EPS) * gn_ref[...]
    h2_ref[0] = (h2 * (1.0 + sc_ref[0]) + sh_ref[0]).astype(h2_ref.dtype)


def _out_proj(a_out, outs, lses, x, mod, gn, w_out, e16, *, tm):
    bsz, seq, d = x.shape
    n_pat = len(outs)
    lm = mod.shape[1]
    mod_rows = tm if lm == seq else 1

    def mod_map(piece):
        if lm == seq:
            return lambda b, i: (b, i, piece)
        return lambda b, i: (b, 0, piece)

    half_spec = pl.BlockSpec((1, tm, D_CONV), lambda b, i: (b, i, 0))
    full_spec = pl.BlockSpec((1, tm, d), lambda b, i: (b, i, 0))
    in_specs = [half_spec]
    for o in outs:
        dil = seq // o.shape[1]
        in_specs.append(pl.BlockSpec((1, tm // dil, dil * D_ATTN), lambda b, i: (b, i, 0)))
    args = [a_out, *outs]
    if n_pat > 1:
        in_specs += [pl.BlockSpec((1, tm, N_HEADS), lambda b, i: (b, i, 0))] * n_pat
        args += list(lses)
    in_specs += [full_spec,
                 pl.BlockSpec((1, mod_rows, d), mod_map(2)),
                 pl.BlockSpec((1, mod_rows, d), mod_map(4)),
                 pl.BlockSpec((1, mod_rows, d), mod_map(3)),
                 pl.BlockSpec((1, d), lambda b, i: (0, 0)),
                 pl.BlockSpec(w_out.shape, lambda b, i: (0, 0)),
                 pl.BlockSpec(e16.shape, lambda b, i: (0, 0))]
    args += [x, mod, mod, mod, gn, w_out, e16]
    return pl.pallas_call(
        functools.partial(_outproj_kernel, n_pat=n_pat),
        grid=(bsz, seq // tm),
        in_specs=in_specs,
        out_specs=[full_spec, full_spec],
        out_shape=[jax.ShapeDtypeStruct((bsz, seq, d), F32), jax.ShapeDtypeStruct((bsz, seq, d), BF16)],
        compiler_params=_params(("parallel", "parallel")),
        name="out_proj",
    )(*args)


def _ffn_kernel(*refs, tm, rs, halo, has_prev, nf):
    (h2_ref, x1_ref, g2_ref, wg_ref, wv_ref, wd_ref, cwg_ref, cwv_ref, cbg_ref, cbv_ref) = refs[:10]
    pos = 10
    prev_refs = (None, None)
    if has_prev:
        prev_refs = refs[pos:pos + 2]
        pos += 2
    y_ref, tg_ref, tv_ref, acc_ref, ug_buf, uv_buf, carry_ref = refs[pos:]
    m = pl.program_id(1)
    f = pl.program_id(2)
    h2 = h2_ref[0]

    def conv_half(w_ref, cw_ref, cb_ref, prev_ref, ubuf, t_ref, slot):
        u = jnp.dot(h2, w_ref[...], preferred_element_type=F32)

        @pl.when(m == 0)
        def _():
            if has_prev:
                ubuf[0:halo, :] = prev_ref[0]
            else:
                ubuf[0:halo, :] = jnp.zeros((halo, u.shape[1]), F32)

        @pl.when(m > 0)
        def _():
            ubuf[0:halo, :] = carry_ref[f, slot]

        ubuf[halo:halo + tm, :] = u
        last = u[tm - halo:, :]
        carry_ref[f, slot] = last
        t_ref[0] = last
        return (cb_ref[...] + cw_ref[0:1, :] * ubuf[halo - 2 * rs:halo - 2 * rs + tm, :]
                + cw_ref[1:2, :] * ubuf[halo - rs:halo - rs + tm, :] + cw_ref[2:3, :] * u)

    gate = conv_half(wg_ref, cwg_ref, cbg_ref, prev_refs[0], ug_buf, tg_ref, 0)
    val = conv_half(wv_ref, cwv_ref, cbv_ref, prev_refs[1], uv_buf, tv_ref, 1)
    act = (gate * _sigmoid(gate) * val).astype(BF16)
    contrib = jnp.dot(act, wd_ref[...], preferred_element_type=F32)

    @pl.when(f == 0)
    def _():
        acc_ref[...] = contrib

    @pl.when(f > 0)
    def _():
        acc_ref[...] += contrib

    @pl.when(f == nf - 1)
    def _():
        y_ref[0] = x1_ref[0] + g2_ref[0] * acc_ref[...]


def _conv_ffn(h2, x1, mod, w_up, w_down, cw, cb, prev, *, tm, tf, rs):
    bsz, seq, d = x1.shape
    nf = D_FF // tf
    has_prev = prev is not None
    halo = 2 * rs if has_prev else SUBLANES
    lm = mod.shape[1]
    mod_rows = tm if lm == seq else 1
    g2_map = (lambda b, m, f: (b, m, 5)) if lm == seq else (lambda b, m, f: (b, 0, 5))
    row = lambda b, m, f: (b, m, 0)
    gate_col = lambda b, m, f: (0, f)
    val_col = lambda b, m, f: (0, nf + f)
    in_specs = [pl.BlockSpec((1, tm, d), row), pl.BlockSpec((1, tm, d), row),
                pl.BlockSpec((1, mod_rows, d), g2_map),
                pl.BlockSpec((d, tf), gate_col), pl.BlockSpec((d, tf), val_col),
                pl.BlockSpec((tf, d), lambda b, m, f: (f, 0)),
                pl.BlockSpec((FFN_CONV_WIDTH, tf), gate_col), pl.BlockSpec((FFN_CONV_WIDTH, tf), val_col),
                pl.BlockSpec((1, tf), gate_col), pl.BlockSpec((1, tf), val_col)]
    args = [h2, x1, mod, w_up, w_up, w_down, cw, cw, cb, cb]
    if has_prev:
        in_specs += [pl.BlockSpec((1, halo, tf), lambda b, m, f: (b, 0, f)),
                     pl.BlockSpec((1, halo, tf), lambda b, m, f: (b, 0, nf + f))]
        args += [prev, prev]
    nm = seq // tm
    tail_spec = pl.BlockSpec((1, halo, tf), lambda b, m, f: (b, m, f))
    tail_shape = jax.ShapeDtypeStruct((bsz, nm * halo, D_FF), F32)
    y, tail_g, tail_v = pl.pallas_call(
        functools.partial(_ffn_kernel, tm=tm, rs=rs, halo=halo, has_prev=has_prev, nf=nf),
        grid=(bsz, nm, nf),
        in_specs=in_specs,
        out_specs=[pl.BlockSpec((1, tm, d), row), tail_spec, tail_spec],
        out_shape=[jax.ShapeDtypeStruct((bsz, seq, d), F32), tail_shape, tail_shape],
        scratch_shapes=[pltpu.VMEM((tm, d), F32),
                        pltpu.VMEM((halo + tm, tf), F32), pltpu.VMEM((halo + tm, tf), F32),
                        pltpu.VMEM((nf, 2, halo, tf), F32)],
        compiler_params=_params(("arbitrary", "arbitrary", "arbitrary")),
        name="conv_ffn",
    )(*args)
    last = (nm - 1) * halo
    return y, jnp.concatenate([tail_g[:, last:], tail_v[:, last:]], axis=-1)


FFN_ROW_CHUNK = 64


def _ffn_pipe_kernel(h2_ref, x1_ref, g2_ref, wg_ref, wv_ref, wd_ref, cwg_ref, cwv_ref, cbg_ref, cbv_ref,
                     y_ref, tg_ref, tv_ref, acc_ref, ua_ref, ub_ref, carry_ref, *, tm, nf, total):
    g = pl.program_id(1)
    gb = jnp.clip(g - 1, 0, total - 1)
    mb = gb // nf
    fb = lax.rem(gb, nf)

    @pl.when(g == 0)
    def _():
        ub_ref[...] = jnp.zeros(ub_ref.shape, F32)
        carry_ref[...] = jnp.zeros(carry_ref.shape, F32)
        acc_ref[...] = jnp.zeros(acc_ref.shape, F32)

    def step(u_rd, u_wr):
        h2 = h2_ref[0]
        u_wr[0] = jnp.dot(h2, wg_ref[...], preferred_element_type=F32)
        u_wr[1] = jnp.dot(h2, wv_ref[...], preferred_element_type=F32)

        prevs = []
        for slot, t_ref in ((0, tg_ref), (1, tv_ref)):
            kept = carry_ref[fb, slot]
            prevs.append(jnp.where(mb == 0, 0.0, kept))
            last = u_rd[slot, tm - SUBLANES:tm, :]
            carry_ref[fb, slot] = jnp.where(g == 0, kept, last)
            t_ref[0] = last

        acts = []
        for r0 in range(0, tm, FFN_ROW_CHUNK):
            def conv_half(slot, cw_ref, cb_ref):
                if r0 == 0:
                    ext = jnp.concatenate([prevs[slot], u_rd[slot, 0:FFN_ROW_CHUNK, :]], axis=0)
                else:
                    ext = u_rd[slot, r0 - SUBLANES:r0 + FFN_ROW_CHUNK, :]
                u = ext[SUBLANES:]
                u1 = pltpu.roll(ext, 1, 0)[SUBLANES:]
                u2 = pltpu.roll(ext, 2, 0)[SUBLANES:]
                return cb_ref[...] + cw_ref[0:1, :] * u2 + cw_ref[1:2, :] * u1 + cw_ref[2:3, :] * u

            gate = conv_half(0, cwg_ref, cbg_ref)
            val = conv_half(1, cwv_ref, cbv_ref)
            acts.append(jnp.where(g == 0, 0.0, gate * _sigmoid(gate) * val).astype(BF16))

        acc_ref[...] += jnp.dot(jnp.concatenate(acts, axis=0), wd_ref[...], preferred_element_type=F32)

    @pl.when(lax.rem(g, 2) == 0)
    def _():
        step(ub_ref, ua_ref)

    @pl.when(lax.rem(g, 2) == 1)
    def _():
        step(ua_ref, ub_ref)

    @pl.when(jnp.logical_and(fb == nf - 1, g >= 1))
    def _():
        y_ref[0] = x1_ref[0] + g2_ref[0] * acc_ref[...]
        acc_ref[...] = jnp.zeros(acc_ref.shape, F32)


def _conv_ffn_prompt(h2, x1, mod, w_up, w_down, cw, cb, *, tm, tf):
    bsz, seq, d = x1.shape
    nf = D_FF // tf
    nm = seq // tm
    total = nm * nf
    up = lambda g: jnp.minimum(g, total - 1)
    mid = lambda g: jnp.clip(g - 1, 0, total - 1)
    tile = lambda s: s // nf
    chunk = lambda s: lax.rem(s, nf)
    in_specs = [pl.BlockSpec((1, tm, d), lambda b, g: (b, tile(up(g)), 0)),
                pl.BlockSpec((1, tm, d), lambda b, g: (b, tile(mid(g)), 0)),
                pl.BlockSpec((1, 1, d), lambda b, g: (b, 0, 5)),
                pl.BlockSpec((d, tf), lambda b, g: (0, chunk(up(g)))),
                pl.BlockSpec((d, tf), lambda b, g: (0, nf + chunk(up(g)))),
                pl.BlockSpec((tf, d), lambda b, g: (chunk(mid(g)), 0)),
                pl.BlockSpec((FFN_CONV_WIDTH, tf), lambda b, g: (0, chunk(mid(g)))),
                pl.BlockSpec((FFN_CONV_WIDTH, tf), lambda b, g: (0, nf + chunk(mid(g)))),
                pl.BlockSpec((1, tf), lambda b, g: (0, chunk(mid(g)))),
                pl.BlockSpec((1, tf), lambda b, g: (0, nf + chunk(mid(g))))]
    tail_spec = pl.BlockSpec((1, SUBLANES, tf), lambda b, g: (b, tile(mid(g)), chunk(mid(g))))
    tail_shape = jax.ShapeDtypeStruct((bsz, nm * SUBLANES, D_FF), F32)
    y, tail_g, tail_v = pl.pallas_call(
        functools.partial(_ffn_pipe_kernel, tm=tm, nf=nf, total=total),
        grid=(bsz, total + 1),
        in_specs=in_specs,
        out_specs=[pl.BlockSpec((1, tm, d), lambda b, g: (b, tile(mid(g)), 0)), tail_spec, tail_spec],
        out_shape=[jax.ShapeDtypeStruct((bsz, seq, d), F32), tail_shape, tail_shape],
        scratch_shapes=[pltpu.VMEM((tm, d), F32),
                        pltpu.VMEM((2, tm, tf), F32), pltpu.VMEM((2, tm, tf), F32),
                        pltpu.VMEM((nf, 2, SUBLANES, tf), F32)],
        compiler_params=_params(("arbitrary", "arbitrary")),
        name="conv_ffn_pipe",
    )(h2, x1, mod, w_up, w_up, w_down, cw, cw, cb, cb)
    last = (nm - 1) * SUBLANES
    return y, jnp.concatenate([tail_g[:, last:], tail_v[:, last:]], axis=-1)


def _rope_tables(pos):
    half = HEAD_DIM // 2
    inv = ROPE_THETA ** (-jnp.arange(half, dtype=F32) / half)
    ang = pos.astype(F32)[:, None] * inv[None, :]
    cos = jnp.cos(ang)
    sin = jnp.sin(ang)
    cos_l = jnp.concatenate([cos, cos, cos, cos], axis=-1)
    sin_l = jnp.concatenate([-sin, sin, -sin, sin], axis=-1)
    return cos_l, sin_l


def _head_mean_matrix():
    h = np.arange(D_ATTN) // HEAD_DIM
    return jnp.asarray((h[:, None] == h[None, :]).astype(np.float32) / HEAD_DIM, dtype=BF16)


def _head_expand_matrix():
    h = np.arange(D_ATTN) // HEAD_DIM
    return jnp.asarray((np.arange(N_HEADS)[:, None] == h[None, :]).astype(np.float32), dtype=BF16)


def kernel(x_prompt, x_sample, cache_win_k, cache_win_v, state_conv_a, state_ffn_conv, c_prompt, c_sample,
           norm_mix_g, norm_ffn_g, w_ada, b_ada, w_in, conv_a_w, conv_a_b, ln_a_g, ln_a_b, q_norm_g, k_norm_g,
           w_out, w_up, ffn_conv_w, ffn_conv_b, w_down):
    bsz, seq, d = x_prompt.shape
    bd, dt, _ = x_sample.shape
    assert w_ada.shape[0] == 1 and dt == DEC_T and cache_win_k.shape[2] == WIN_MAX and seq == 2 * WIN_MAX

    w_in_b = w_in[0].astype(BF16)
    w_out_b = w_out[0].astype(BF16)
    w_up_b = w_up[0].astype(BF16)
    w_down_b = w_down[0].astype(BF16)
    e_mean = _head_mean_matrix()
    e16 = _head_expand_matrix()
    qg = jnp.tile(q_norm_g[0], N_HEADS)[None, :]
    kg = jnp.tile(k_norm_g[0], N_HEADS)[None, :]

    c_all = jnp.concatenate([c_prompt, c_sample, jnp.zeros((16 - bsz - bd, d), F32)], axis=0)
    mod = _ada(c_all, w_ada[0], b_ada)
    mod_p = mod[:bsz][:, None, :]
    mod_s = jnp.tile(mod[bsz:bsz + bd], (dt, 1))[None]

    cos_p, sin_p = _rope_tables(jnp.arange(seq))
    g_p, *qkv_views, g_tail, k_tail, v_tail = _in_proj(
        x_prompt, mod_p, mod_p, norm_mix_g, w_in_b, qg, kg, cos_p, sin_p, e_mean,
        tm=256, out_dtype=BF16, tails=True)
    a_p = _conv_mix(g_p, None, conv_a_w[0], conv_a_b, ln_a_g, ln_a_b, tc=256, rs=1, out_dtype=BF16)
    outs, lses = [], []
    for n, ((window, dil), width, ub) in enumerate(zip(DIL_PATTERNS, (256, 512, 512), (2, 1, 1))):
        o, lse = _band_attn(*qkv_views[3 * n:3 * n + 3], dil, width, ub)
        outs.append(o)
        lses.append(lse)
    x1_p, h2_p = _out_proj(a_p, outs, lses, x_prompt, mod_p, norm_ffn_g, w_out_b, e16, tm=256)
    y_p, ftail_p = _conv_ffn_prompt(h2_p, x1_p, mod_p, w_up_b, w_down_b, ffn_conv_w[0], ffn_conv_b,
                                    tm=512, tf=512)

    rows = dt * bd
    to_tm = lambda a: a.transpose(1, 0, 2).reshape(1, a.shape[0] * a.shape[1], a.shape[2])
    xs = to_tm(x_sample)
    cos_s, sin_s = _rope_tables(PAST_LEN + jnp.arange(rows) // bd)
    g_s, q_s, k_s, v_s = _in_proj(xs, mod_s, mod_s, norm_mix_g, w_in_b, qg, kg, cos_s, sin_s, e_mean,
                                  tm=rows, out_dtype=F32, tails=False)
    a_s = _conv_mix(g_s, to_tm(state_conv_a[0]), conv_a_w[0], conv_a_b, ln_a_g, ln_a_b,
                    tc=rows, rs=bd, out_dtype=BF16)
    heads = lambda a: a.reshape(dt, bd, N_HEADS, HEAD_DIM)
    ng = N_HEADS // HEAD_GROUP
    qg5 = heads(q_s).reshape(dt, bd, ng, HEAD_GROUP, HEAD_DIM).transpose(1, 2, 3, 0, 4)
    q_blk = (qg5[:, :, :, :, None, :] * jnp.eye(HEAD_GROUP, dtype=F32)[None, None, :, None, :, None])
    q_blk = q_blk.reshape(bd, ng, HEAD_GROUP * dt, GROUP_COLS).astype(BF16)
    to_pos_minor = lambda c: c[0].transpose(0, 2, 3, 1).reshape(bd, D_ATTN, WIN_MAX)
    new_tile = lambda a: jnp.pad(heads(a).transpose(1, 2, 3, 0).reshape(bd, D_ATTN, dt),
                                 ((0, 0), (0, 0), (NEW_LANE0, 0)))
    o_s, win_kt, win_vt = _sample_attn(q_blk, to_pos_minor(cache_win_k), to_pos_minor(cache_win_v),
                                       new_tile(k_s), new_tile(v_s))
    o_s = o_s.transpose(1, 0, 2).reshape(1, rows, D_ATTN)
    from_pos_minor = lambda a: a.reshape(a.shape[0], N_HEADS, HEAD_DIM, a.shape[-1]).transpose(0, 3, 1, 2)[None]
    win_k_s, win_v_s = from_pos_minor(win_kt), from_pos_minor(win_vt)
    x1_s, h2_s = _out_proj(a_s, [o_s], None, xs, mod_s, norm_ffn_g, w_out_b, e16, tm=rows)
    y_s, ftail_s = _conv_ffn(h2_s, x1_s, mod_s, w_up_b, w_down_b, ffn_conv_w[0], ffn_conv_b,
                             to_tm(state_ffn_conv[0]), tm=rows, tf=512, rs=bd)

    from_tm = lambda a, t: a.reshape(t, bd, a.shape[-1]).transpose(1, 0, 2)
    conv_a_s = jnp.concatenate([state_conv_a[0], from_tm(g_s[0], dt)], axis=1)[:, dt:]
    return (y_p, from_tm(y_s[0], dt),
            from_pos_minor(k_tail), from_pos_minor(v_tail),
            g_tail[:, 32 - (CONV_A_WIDTH - 1):][None],
            ftail_p[:, SUBLANES - (FFN_CONV_WIDTH - 1):][None],
            win_k_s, win_v_s,
            conv_a_s[None],
            from_tm(ftail_s[0], FFN_CONV_WIDTH - 1)[None])
```

```python
import functools

import numpy as np
import jax
import jax.numpy as jnp
from jax import lax
from jax.experimental import pallas as pl
from jax.experimental.pallas import tpu as pltpu

F32 = jnp.float32
BF16 = jnp.bfloat16

D_MODEL = 2048
HEAD_DIM = 64
D_CONV = D_MODEL // 2
N_HEADS = (D_MODEL // 2) // HEAD_DIM
D_ATTN = N_HEADS * HEAD_DIM
CONV_A_WIDTH = 31
DIL_PATTERNS = ((128, 1), (512, 4), (2048, 16))
BAND = 128
WIN_MAX = 2048
ROPE_THETA = 10000.0
D_FF = 5632
FFN_CONV_WIDTH = 3
EPS = 1e-6
NEG_INF = -1e30
PAST_LEN = 16384

LANES = 128
SUBLANES = 8
VMEM_LIMIT = 56 * 1024 * 1024
IN_PROJ_VMEM_LIMIT = 60 * 1024 * 1024


def _sigmoid(x):
    return 1.0 / (1.0 + jnp.exp(-x))


def _params(sem, vmem=VMEM_LIMIT, flags=None):
    return pltpu.CompilerParams(dimension_semantics=sem, vmem_limit_bytes=vmem, flags=flags)


STAGE_COLS = 256


def _stage_weight_bf16(w_hbm, w_bf16, stage, sem):
    chunks = w_hbm.shape[1] // STAGE_COLS

    def copy(j):
        return pltpu.make_async_copy(w_hbm.at[:, pl.ds(j * STAGE_COLS, STAGE_COLS)],
                                     stage.at[j % 2], sem.at[j % 2])

    copy(0).start()
    for j in range(chunks):
        if j + 1 < chunks:
            copy(j + 1).start()
        copy(j).wait()
        w_bf16[:, j * STAGE_COLS:(j + 1) * STAGE_COLS] = stage[j % 2].astype(BF16)


def _weight_scratch(w):
    return [pltpu.VMEM(w.shape, BF16), pltpu.VMEM((2, w.shape[0], STAGE_COLS), F32),
            pltpu.SemaphoreType.DMA((2,))]


def _ada_kernel(c_ref, w_ref, b_ref, o_ref):
    c = c_ref[...]
    s = (c * _sigmoid(c)).astype(BF16)
    o_ref[...] = jnp.dot(s, w_ref[...].astype(BF16), preferred_element_type=F32) + b_ref[...]


def _ada(c_all, w_ada, b_ada):
    rows, d = c_all.shape
    n = w_ada.shape[1]
    tn = 1024
    return pl.pallas_call(
        _ada_kernel,
        grid=(n // tn,),
        in_specs=[pl.BlockSpec((rows, d), lambda j: (0, 0)),
                  pl.BlockSpec((d, tn), lambda j: (0, j)),
                  pl.BlockSpec((1, tn), lambda j: (0, j))],
        out_specs=pl.BlockSpec((rows, tn), lambda j: (0, j)),
        out_shape=jax.ShapeDtypeStruct((rows, n), F32),
        compiler_params=_params(("parallel",)),
        name="ada_mod",
    )(c_all, w_ada, b_ada)


def _head_norm_rope(x, gain, cos, sin, e):
    ms = jnp.dot((x * x).astype(BF16), e, preferred_element_type=F32)
    xn = x * lax.rsqrt(ms + EPS) * gain
    lane = lax.broadcasted_iota(jnp.int32, (1, LANES), 1)
    first_half = jnp.bitwise_and(lane, HEAD_DIM - 1) < (HEAD_DIM // 2)
    out = []
    for c in range(D_ATTN // LANES):
        xc = xn[:, c * LANES:(c + 1) * LANES]
        partner = jnp.where(first_half, pltpu.roll(xc, LANES - HEAD_DIM // 2, 1),
                            pltpu.roll(xc, HEAD_DIM // 2, 1))
        out.append(xc * cos + partner * sin)
    return out


def _inproj_kernel(x_ref, sc_ref, sh_ref, gmix_ref, w_hbm, qg_ref, kg_ref, cos_ref, sin_ref, e_ref,
                   g_ref, q_ref, k_ref, v_ref, *rest, nt, prompt):
    extra_refs = rest[:-3]
    w_ref, stage_ref, sem = rest[-3:]

    @pl.when(jnp.logical_and(pl.program_id(0) == 0, pl.program_id(1) == 0))
    def _():
        _stage_weight_bf16(w_hbm, w_ref, stage_ref, sem)

    x = x_ref[0]
    ms = jnp.mean(x * x, axis=-1, keepdims=True)
    h = x * lax.rsqrt(ms + EPS) * gmix_ref[...]
    hb = (h * (1.0 + sc_ref[0]) + sh_ref[0]).astype(BF16)

    def proj(j):
        return jnp.dot(hb, w_ref[:, j * D_CONV:(j + 1) * D_CONV], preferred_element_type=F32)

    g = proj(0) * _sigmoid(proj(1))
    g_ref[0] = g.astype(g_ref.dtype)
    cos = cos_ref[...]
    sin = sin_ref[...]
    e = e_ref[...]
    q = jnp.concatenate(_head_norm_rope(proj(2), qg_ref[...], cos, sin, e), axis=1) * (HEAD_DIM ** -0.5)
    k = jnp.concatenate(_head_norm_rope(proj(3), kg_ref[...], cos, sin, e), axis=1)
    v = proj(4)
    qkv = [a.astype(q_ref.dtype) for a in (q, k, v)]
    for ref, a in zip((q_ref, k_ref, v_ref), qkv):
        ref[0] = a

    if prompt:
        view_refs = extra_refs[:6]
        gt_ref, kt_ref, vt_ref = extra_refs[6:]
        tm = x.shape[0]
        for n, (_, dil) in enumerate(DIL_PATTERNS[1:]):
            for ref, a in zip(view_refs[3 * n:3 * n + 3], qkv):
                ref[0] = a.reshape(tm // dil, dil * D_ATTN)
        i = pl.program_id(1)
        rows = gt_ref.shape[1]

        @pl.when(i == nt - 1)
        def _():
            gt_ref[0] = g[g.shape[0] - rows:, :]

        @pl.when(i >= nt // 2)
        def _():
            kt_ref[0] = k.T
            vt_ref[0] = v.T


def _in_proj(x, sc, sh, gmix, w_in, qg, kg, cos, sin, e, *, tm, out_dtype, tails):
    bsz, seq, d = x.shape
    nt = seq // tm
    lm = sc.shape[1]
    mod_rows = tm if lm == seq else 1

    def mod_map(piece):
        if lm == seq:
            return lambda b, i: (b, i, piece)
        return lambda b, i: (b, 0, piece)

    row_spec = pl.BlockSpec((1, tm, D_CONV), lambda b, i: (b, i, 0))
    in_specs = [
        pl.BlockSpec((1, tm, d), lambda b, i: (b, i, 0)),
        pl.BlockSpec((1, mod_rows, d), mod_map(1)),
        pl.BlockSpec((1, mod_rows, d), mod_map(0)),
        pl.BlockSpec((1, d), lambda b, i: (0, 0)),
        pl.BlockSpec(memory_space=pl.ANY),
        pl.BlockSpec((1, D_ATTN), lambda b, i: (0, 0)),
        pl.BlockSpec((1, D_ATTN), lambda b, i: (0, 0)),
        pl.BlockSpec((tm, LANES), lambda b, i: (i, 0)),
        pl.BlockSpec((tm, LANES), lambda b, i: (i, 0)),
        pl.BlockSpec(e.shape, lambda b, i: (0, 0)),
    ]
    out_specs = [row_spec] * 4
    out_shape = [jax.ShapeDtypeStruct((bsz, seq, D_CONV), out_dtype)] * 4
    if tails:
        half = nt // 2
        tail_map = lambda b, i: (b, 0, jnp.maximum(i - half, 0))
        for _, dil in DIL_PATTERNS[1:]:
            out_specs = out_specs + [pl.BlockSpec((1, tm // dil, dil * D_ATTN), lambda b, i: (b, i, 0))] * 3
            out_shape = out_shape + [jax.ShapeDtypeStruct((bsz, seq // dil, dil * D_ATTN), out_dtype)] * 3
        out_specs = out_specs + [pl.BlockSpec((1, 32, D_CONV), lambda b, i: (b, 0, 0)),
                                 pl.BlockSpec((1, D_ATTN, tm), tail_map),
                                 pl.BlockSpec((1, D_ATTN, tm), tail_map)]
        out_shape = out_shape + [jax.ShapeDtypeStruct((bsz, 32, D_CONV), F32),
                                 jax.ShapeDtypeStruct((bsz, D_ATTN, seq // 2), F32),
                                 jax.ShapeDtypeStruct((bsz, D_ATTN, seq // 2), F32)]
    return pl.pallas_call(
        functools.partial(_inproj_kernel, nt=nt, prompt=tails),
        grid=(bsz, nt),
        in_specs=in_specs,
        out_specs=out_specs,
        out_shape=out_shape,
        scratch_shapes=_weight_scratch(w_in),
        compiler_params=_params(("arbitrary", "arbitrary"), IN_PROJ_VMEM_LIMIT),
        name="in_proj",
    )(x, sc, sh, gmix, w_in, qg, kg, cos, sin, e)


def _conv_kernel(*refs, tc, rs, halo, has_prev, rc):
    if has_prev:
        g_ref, prev_ref, w_ref, b_ref, lg_ref, lb_ref, o_ref, buf = refs
    else:
        g_ref, w_ref, b_ref, lg_ref, lb_ref, o_ref, buf = refs
    i = pl.program_id(1)

    @pl.when(i == 0)
    def _():
        if has_prev:
            buf[0:halo, :] = prev_ref[0]
        else:
            buf[0:halo, :] = jnp.zeros((halo, D_CONV), F32)

    @pl.when(i > 0)
    def _():
        buf[0:halo, :] = buf[tc:tc + halo, :]

    buf[halo:halo + tc, :] = g_ref[0].astype(F32)
    base = halo - (CONV_A_WIDTH - 1) * rs
    bias = b_ref[...]
    lg = lg_ref[...]
    lb = lb_ref[...]
    for c in range(tc // rc):
        r0 = c * rc
        acc = jnp.broadcast_to(bias, (rc, D_CONV))
        if rs % SUBLANES == 0:
            for k in range(CONV_A_WIDTH):
                off = r0 + base + k * rs
                acc = acc + w_ref[k:k + 1, :] * buf[off:off + rc, :]
        else:
            n = rc + halo
            ext = buf[r0:r0 + n, :]
            for s in range(SUBLANES):
                rolled = ext if s == 0 else pltpu.roll(ext, n - s, 0)
                for k in range(CONV_A_WIDTH):
                    off = base + k * rs
                    if off % SUBLANES == s:
                        acc = acc + w_ref[k:k + 1, :] * rolled[off - s:off - s + rc, :]
        mu = jnp.mean(acc, axis=-1, keepdims=True)
        xc = acc - mu
        var = jnp.mean(xc * xc, axis=-1, keepdims=True)
        y = xc * lax.rsqrt(var + EPS) * lg + lb
        o_ref[0, r0:r0 + rc, :] = (y * _sigmoid(y)).astype(o_ref.dtype)


def _conv_mix(g, prev, w, b, lg, lb, *, tc, rs, out_dtype):
    bsz, seq, ch = g.shape
    has_prev = prev is not None
    halo = prev.shape[1] if has_prev else 32
    rc = min(tc, 32)
    in_specs = [pl.BlockSpec((1, tc, ch), lambda bb, i: (bb, i, 0))]
    args = [g]
    if has_prev:
        in_specs.append(pl.BlockSpec((1, halo, ch), lambda bb, i: (bb, 0, 0)))
        args.append(prev)
    in_specs += [pl.BlockSpec(w.shape, lambda bb, i: (0, 0))] + [pl.BlockSpec((1, ch), lambda bb, i: (0, 0))] * 3
    args += [w, b, lg, lb]
    return pl.pallas_call(
        functools.partial(_conv_kernel, tc=tc, rs=rs, halo=halo, has_prev=has_prev, rc=rc),
        grid=(bsz, seq // tc),
        in_specs=in_specs,
        out_specs=pl.BlockSpec((1, tc, ch), lambda bb, i: (bb, i, 0)),
        out_shape=jax.ShapeDtypeStruct((bsz, seq, ch), out_dtype),
        scratch_shapes=[pltpu.VMEM((halo + tc, ch), F32)],
        compiler_params=_params(("arbitrary", "arbitrary")),
        name="conv_mix",
    )(*args)


PAIRS_PER_PHASE = 4


def _band_attn_kernel(q_ref, k_ref, v_ref, o_ref, lse_ref, *, lr, hb, ub):
    nblk = lr // BAND
    qi = lax.broadcasted_iota(jnp.int32, (BAND, 2 * BAND), 0)
    kj = lax.broadcasted_iota(jnp.int32, (BAND, 2 * BAND), 1)
    bias_band = jnp.where(kj >= qi, jnp.where(kj <= qi + BAND, 0.0, NEG_INF), NEG_INF).astype(F32)
    qi1 = lax.broadcasted_iota(jnp.int32, (BAND, BAND), 0)
    kj1 = lax.broadcasted_iota(jnp.int32, (BAND, BAND), 1)
    bias_first = jnp.where(kj1 <= qi1, 0.0, NEG_INF).astype(F32)

    lane = lax.broadcasted_iota(jnp.int32, (1, LANES), 1)
    even_lanes = lane < HEAD_DIM

    def blocks(specs):
        units = [(q0, k0, nk, bias, pr) for (q0, k0, nk, bias) in specs for pr in range(hb // 2)]
        for g in range(0, len(units), PAIRS_PER_PHASE):
            phase_group(units[g:g + PAIRS_PER_PHASE])

    def phase_group(units):
        scores = []
        for q0, k0, nk, bias, pr in units:
            cols = slice(pr * LANES, (pr + 1) * LANES)
            qp = q_ref[0, pl.ds(q0, BAND), cols]
            kp = k_ref[0, pl.ds(k0, nk), cols]
            q2 = jnp.concatenate([jnp.where(even_lanes, qp, 0), jnp.where(even_lanes, 0, qp)], axis=0)
            s = lax.dot_general(q2, kp, (((1,), (1,)), ((), ())), preferred_element_type=F32)
            scores.append(s + jnp.concatenate([bias, bias], axis=0))
        probs = []
        for s in scores:
            m = jnp.max(s, axis=-1, keepdims=True)
            probs.append((jnp.exp(s - m).astype(BF16), m))
        for (q0, k0, nk, bias, pr), (p, m) in zip(units, probs):
            cols = slice(pr * LANES, (pr + 1) * LANES)
            vp = v_ref[0, pl.ds(k0, nk), cols]
            one = jnp.ones((), vp.dtype)
            r_even = jnp.dot(p[:BAND], jnp.where(even_lanes, vp, one), preferred_element_type=F32)
            r_odd = jnp.dot(p[BAND:], jnp.where(even_lanes, one, vp), preferred_element_type=F32)
            num = jnp.where(even_lanes, r_even, r_odd)
            den = pltpu.roll(jnp.where(even_lanes, r_odd, r_even), HEAD_DIM, 1)
            o_ref[0, pl.ds(q0, BAND), cols] = (num * (1.0 / den)).astype(o_ref.dtype)
            log_den = jnp.log(den)
            lse_ref[0, 0, 0, pl.ds(q0, BAND), 2 * pr:2 * pr + 1] = m[:BAND] + log_den[:, 0:1]
            lse_ref[0, 0, 0, pl.ds(q0, BAND), 2 * pr + 1:2 * pr + 2] = m[BAND:] + log_den[:, HEAD_DIM:HEAD_DIM + 1]

    blocks([(0, 0, BAND, bias_first)])

    def banded(i):
        return (pl.multiple_of(i * BAND, BAND), pl.multiple_of((i - 1) * BAND, BAND), 2 * BAND, bias_band)

    groups = (nblk - 1) // ub

    def body(gi, carry):
        blocks([banded(1 + gi * ub + j) for j in range(ub)])
        return carry

    lax.fori_loop(0, groups, body, 0)
    rest = list(range(1 + groups * ub, nblk))
    if rest:
        blocks([banded(i) for i in rest])


def _band_attn(q, k, v, dil, width, ub):
    bsz, lr, cols = q.shape
    da = cols // dil
    seq = lr * dil
    hb = width // HEAD_DIM
    ncol = da // width
    spec = pl.BlockSpec((1, lr, width), lambda b, c: (b, 0, c))
    o, lse = pl.pallas_call(
        functools.partial(_band_attn_kernel, lr=lr, hb=hb, ub=ub),
        grid=(bsz, dil * ncol),
        in_specs=[spec, spec, spec],
        out_specs=[spec,
                   pl.BlockSpec((1, 1, 1, lr, hb), lambda b, c: (b, c // ncol, c % ncol, 0, 0))],
        out_shape=[jax.ShapeDtypeStruct((bsz, lr, cols), BF16),
                   jax.ShapeDtypeStruct((bsz, dil, ncol, lr, hb), F32)],
        compiler_params=_params(("parallel", "parallel")),
        name=f"band_attn_d{dil}",
    )(q, k, v)
    lse = lse.transpose(0, 3, 1, 2, 4).reshape(bsz, seq, N_HEADS)
    return o, lse


DEC_T = 4
HEAD_GROUP = 4
GROUP_COLS = HEAD_GROUP * HEAD_DIM
NEW_LANE0 = LANES - DEC_T


def _sample_mult_tables():
    t = np.arange(HEAD_GROUP * DEC_T) % DEC_T

    def table(dist):
        m = np.zeros(dist.shape, np.float32)
        for window, dil in DIL_PATTERNS:
            m += ((dist >= 0) & (dist <= window) & (dist % dil == 0)).astype(np.float32)
        return m

    win = table(WIN_MAX + t[:, None] - np.arange(WIN_MAX)[None, :])
    lane = np.arange(LANES)
    new = table(np.where(lane >= NEW_LANE0, t[:, None] - (lane[None, :] - NEW_LANE0), -1))
    row_head = np.arange(HEAD_GROUP * DEC_T) // DEC_T
    own = (row_head[:, None] == (np.arange(GROUP_COLS) // HEAD_DIM)[None, :]).astype(np.float32)
    return win, new, own


def _sattn_kernel(q_ref, kt_ref, vt_ref, kn_ref, vn_ref, mw_ref, mn_ref, own_ref, o_ref, okt_ref, ovt_ref):
    q = q_ref[...]
    kt = kt_ref[...]
    kn = kn_ref[...]
    mw = mw_ref[...]
    mn = mn_ref[...]
    s_win = jnp.where(mw > 0.0, jnp.dot(q, kt.astype(BF16), preferred_element_type=F32), NEG_INF)
    s_new = jnp.where(mn > 0.0, jnp.dot(q, kn.astype(BF16), preferred_element_type=F32), NEG_INF)
    m = jnp.maximum(jnp.max(s_win, axis=-1, keepdims=True), jnp.max(s_new, axis=-1, keepdims=True))
    p_win = mw * jnp.exp(s_win - m)
    p_new = mn * jnp.exp(s_new - m)
    den = jnp.sum(p_win, axis=-1, keepdims=True) + jnp.sum(p_new, axis=-1, keepdims=True)
    vt = vt_ref[...]
    vn = vn_ref[...]
    contract_lanes = (((1,), (1,)), ((), ()))
    o = (lax.dot_general(p_win.astype(BF16), vt.astype(BF16), contract_lanes, preferred_element_type=F32)
         + lax.dot_general(p_new.astype(BF16), vn.astype(BF16), contract_lanes, preferred_element_type=F32))
    o = o * (1.0 / den) * own_ref[...]
    o_ref[...] = functools.reduce(lambda a, c: a + c,
                                  [o[h * DEC_T:(h + 1) * DEC_T] for h in range(HEAD_GROUP)])

    lane = lax.broadcasted_iota(jnp.int32, kn.shape, 1)
    last = WIN_MAX - LANES
    for src, new, dst in ((kt, kn, okt_ref), (vt, vn, ovt_ref)):
        rolled = pltpu.roll(src, WIN_MAX - DEC_T, 1)
        dst[...] = rolled
        dst[:, last:] = jnp.where(lane >= NEW_LANE0, new, rolled[:, last:])


def _sample_attn(q_blk, kt, vt, kn, vn):
    bd = kt.shape[0]
    ng = N_HEADS // HEAD_GROUP
    win, new, own = (jnp.asarray(t) for t in _sample_mult_tables())
    win_spec = pl.BlockSpec((None, GROUP_COLS, WIN_MAX), lambda b, g: (b, g, 0))
    new_spec = pl.BlockSpec((None, GROUP_COLS, LANES), lambda b, g: (b, g, 0))
    const = lambda a: pl.BlockSpec(a.shape, lambda b, g: (0, 0))
    return pl.pallas_call(
        _sattn_kernel,
        grid=(bd, ng),
        in_specs=[pl.BlockSpec((None, None, HEAD_GROUP * DEC_T, GROUP_COLS), lambda b, g: (b, g, 0, 0)),
                  win_spec, win_spec, new_spec, new_spec, const(win), const(new), const(own)],
        out_specs=[pl.BlockSpec((None, DEC_T, GROUP_COLS), lambda b, g: (b, 0, g)), win_spec, win_spec],
        out_shape=[jax.ShapeDtypeStruct((bd, DEC_T, D_ATTN), F32),
                   jax.ShapeDtypeStruct(kt.shape, kt.dtype),
                   jax.ShapeDtypeStruct(vt.shape, vt.dtype)],
        compiler_params=_params(("parallel", "parallel")),
        name="sample_attn",
    )(q_blk, kt, vt, kn, vn, win, new, own)


def _outproj_kernel(*refs, n_pat):
    a_ref = refs[0]
    o_refs = refs[1:1 + n_pat]
    pos = 1 + n_pat
    lse_refs = ()
    if n_pat > 1:
        lse_refs = refs[pos:pos + n_pat]
        pos += n_pat
    x_ref, g1_ref, sc_ref, sh_ref, gn_ref, w_hbm, e_ref, x1_ref, h2_ref, w_ref, stage_ref, sem = refs[pos:]

    @pl.when(jnp.logical_and(pl.program_id(0) == 0, pl.program_id(1) == 0))
    def _():
        _stage_weight_bf16(w_hbm, w_ref, stage_ref, sem)

    if n_pat > 1:
        lses = [r[0] for r in lse_refs]
        mx = functools.reduce(jnp.maximum, lses)
        es = [jnp.exp(l - mx) for l in lses]
        inv = 1.0 / functools.reduce(lambda a, c: a + c, es)
        e16 = e_ref[...]
        o = None
        for ep, o_ref in zip(es, o_refs):
            wp = ep * inv
            hi = wp.astype(BF16)
            lo = (wp - hi.astype(F32)).astype(BF16)
            wb = (jnp.dot(hi, e16, preferred_element_type=F32)
                  + jnp.dot(lo, e16, preferred_element_type=F32))
            term = wb * o_ref[0].reshape(wb.shape).astype(F32)
            o = term if o is None else o + term
        ob = o.astype(BF16)
    else:
        ob = o_refs[0][0].astype(BF16)

    mix = (jnp.dot(a_ref[0].astype(BF16), w_ref[0:D_CONV, :], preferred_element_type=F32)
           + jnp.dot(ob, w_ref[D_CONV:, :], preferred_element_type=F32))
    x1 = x_ref[0] + g1_ref[0] * mix
    x1_ref[0] = x1
    ms = jnp.mean(x1 * x1, axis=-1, keepdims=True)
    h2 = x1 * lax.rsqrt(ms + EPS) * gn_ref[...]
    h2_ref[0] = (h2 * (1.0 + sc_ref[0]) + sh_ref[0]).astype(h2_ref.dtype)


def _out_proj(a_out, outs, lses, x, mod, gn, w_out, e16, *, tm):
    bsz, seq, d = x.shape
    n_pat = len(outs)
    lm = mod.shape[1]
    mod_rows = tm if lm == seq else 1

    def mod_map(piece):
        if lm == seq:
            return lambda b, i: (b, i, piece)
        return lambda b, i: (b, 0, piece)

    half_spec = pl.BlockSpec((1, tm, D_CONV), lambda b, i: (b, i, 0))
    full_spec = pl.BlockSpec((1, tm, d), lambda b, i: (b, i, 0))
    in_specs = [half_spec]
    for o in outs:
        dil = seq // o.shape[1]
        in_specs.append(pl.BlockSpec((1, tm // dil, dil * D_ATTN), lambda b, i: (b, i, 0)))
    args = [a_out, *outs]
    if n_pat > 1:
        in_specs += [pl.BlockSpec((1, tm, N_HEADS), lambda b, i: (b, i, 0))] * n_pat
        args += list(lses)
    in_specs += [full_spec,
                 pl.BlockSpec((1, mod_rows, d), mod_map(2)),
                 pl.BlockSpec((1, mod_rows, d), mod_map(4)),
                 pl.BlockSpec((1, mod_rows, d), mod_map(3)),
                 pl.BlockSpec((1, d), lambda b, i: (0, 0)),
                 pl.BlockSpec(memory_space=pl.ANY),
                 pl.BlockSpec(e16.shape, lambda b, i: (0, 0))]
    args += [x, mod, mod, mod, gn, w_out, e16]
    return pl.pallas_call(
        functools.partial(_outproj_kernel, n_pat=n_pat),
        grid=(bsz, seq // tm),
        in_specs=in_specs,
        out_specs=[full_spec, full_spec],
        out_shape=[jax.ShapeDtypeStruct((bsz, seq, d), F32), jax.ShapeDtypeStruct((bsz, seq, d), BF16)],
        scratch_shapes=_weight_scratch(w_out),
        compiler_params=_params(("arbitrary", "arbitrary")),
        name="out_proj",
    )(*args)


def _ffn_kernel(*refs, tm, rs, halo, has_prev, nf):
    (h2_ref, x1_ref, g2_ref, wg_ref, wv_ref, wd_ref, cwg_ref, cwv_ref, cbg_ref, cbv_ref) = refs[:10]
    pos = 10
    prev_refs = (None, None)
    if has_prev:
        prev_refs = refs[pos:pos + 2]
        pos += 2
    y_ref, tg_ref, tv_ref, acc_ref, ug_buf, uv_buf, carry_ref = refs[pos:]
    m = pl.program_id(1)
    f = pl.program_id(2)
    h2 = h2_ref[0]

    def conv_half(w_ref, cw_ref, cb_ref, prev_ref, ubuf, t_ref, slot):
        u = jnp.dot(h2, w_ref[...], preferred_element_type=F32)

        @pl.when(m == 0)
        def _():
            if has_prev:
                ubuf[0:halo, :] = prev_ref[0]
            else:
                ubuf[0:halo, :] = jnp.zeros((halo, u.shape[1]), F32)

        @pl.when(m > 0)
        def _():
            ubuf[0:halo, :] = carry_ref[f, slot]

        ubuf[halo:halo + tm, :] = u
        last = u[tm - halo:, :]
        carry_ref[f, slot] = last
        t_ref[0] = last
        return (cb_ref[...] + cw_ref[0:1, :] * ubuf[halo - 2 * rs:halo - 2 * rs + tm, :]
                + cw_ref[1:2, :] * ubuf[halo - rs:halo - rs + tm, :] + cw_ref[2:3, :] * u)

    gate = conv_half(wg_ref, cwg_ref, cbg_ref, prev_refs[0], ug_buf, tg_ref, 0)
    val = conv_half(wv_ref, cwv_ref, cbv_ref, prev_refs[1], uv_buf, tv_ref, 1)
    act = (gate * _sigmoid(gate) * val).astype(BF16)
    contrib = jnp.dot(act, wd_ref[...], preferred_element_type=F32)

    @pl.when(f == 0)
    def _():
        acc_ref[...] = contrib

    @pl.when(f > 0)
    def _():
        acc_ref[...] += contrib

    @pl.when(f == nf - 1)
    def _():
        y_ref[0] = x1_ref[0] + g2_ref[0] * acc_ref[...]


def _conv_ffn(h2, x1, mod, w_up, w_down, cw, cb, prev, *, tm, tf, rs):
    bsz, seq, d = x1.shape
    nf = D_FF // tf
    has_prev = prev is not None
    halo = 2 * rs if has_prev else SUBLANES
    lm = mod.shape[1]
    mod_rows = tm if lm == seq else 1
    g2_map = (lambda b, m, f: (b, m, 5)) if lm == seq else (lambda b, m, f: (b, 0, 5))
    row = lambda b, m, f: (b, m, 0)
    gate_col = lambda b, m, f: (0, f)
    val_col = lambda b, m, f: (0, nf + f)
    in_specs = [pl.BlockSpec((1, tm, d), row), pl.BlockSpec((1, tm, d), row),
                pl.BlockSpec((1, mod_rows, d), g2_map),
                pl.BlockSpec((d, tf), gate_col), pl.BlockSpec((d, tf), val_col),
                pl.BlockSpec((tf, d), lambda b, m, f: (f, 0)),
                pl.BlockSpec((FFN_CONV_WIDTH, tf), gate_col), pl.BlockSpec((FFN_CONV_WIDTH, tf), val_col),
                pl.BlockSpec((1, tf), gate_col), pl.BlockSpec((1, tf), val_col)]
    args = [h2, x1, mod, w_up, w_up, w_down, cw, cw, cb, cb]
    if has_prev:
        in_specs += [pl.BlockSpec((1, halo, tf), lambda b, m, f: (b, 0, f)),
                     pl.BlockSpec((1, halo, tf), lambda b, m, f: (b, 0, nf + f))]
        args += [prev, prev]
    nm = seq // tm
    tail_spec = pl.BlockSpec((1, halo, tf), lambda b, m, f: (b, m, f))
    tail_shape = jax.ShapeDtypeStruct((bsz, nm * halo, D_FF), F32)
    y, tail_g, tail_v = pl.pallas_call(
        functools.partial(_ffn_kernel, tm=tm, rs=rs, halo=halo, has_prev=has_prev, nf=nf),
        grid=(bsz, nm, nf),
        in_specs=in_specs,
        out_specs=[pl.BlockSpec((1, tm, d), row), tail_spec, tail_spec],
        out_shape=[jax.ShapeDtypeStruct((bsz, seq, d), F32), tail_shape, tail_shape],
        scratch_shapes=[pltpu.VMEM((tm, d), F32),
                        pltpu.VMEM((halo + tm, tf), F32), pltpu.VMEM((halo + tm, tf), F32),
                        pltpu.VMEM((nf, 2, halo, tf), F32)],
        compiler_params=_params(("arbitrary", "arbitrary", "arbitrary")),
        name="conv_ffn",
    )(*args)
    last = (nm - 1) * halo
    return y, jnp.concatenate([tail_g[:, last:], tail_v[:, last:]], axis=-1)


FFN_ROW_CHUNK = 64


def _ffn_pipe_kernel(h2_ref, x1_ref, g2_ref, wg_ref, wv_ref, wd_ref, cwg_ref, cwv_ref, cbg_ref, cbv_ref,
                     y_ref, tg_ref, tv_ref, acc_ref, ua_ref, ub_ref, carry_ref, *, tm, nf, total):
    g = pl.program_id(1)
    gb = jnp.clip(g - 1, 0, total - 1)
    mb = gb // nf
    fb = lax.rem(gb, nf)

    @pl.when(g == 0)
    def _():
        ub_ref[...] = jnp.zeros(ub_ref.shape, F32)
        carry_ref[...] = jnp.zeros(carry_ref.shape, F32)
        acc_ref[...] = jnp.zeros(acc_ref.shape, F32)

    def step(u_rd, u_wr):
        h2 = h2_ref[0]
        u_wr[0] = jnp.dot(h2, wg_ref[...], preferred_element_type=F32)
        u_wr[1] = jnp.dot(h2, wv_ref[...], preferred_element_type=F32)

        prevs = []
        for slot, t_ref in ((0, tg_ref), (1, tv_ref)):
            kept = carry_ref[fb, slot]
            prevs.append(jnp.where(mb == 0, 0.0, kept))
            last = u_rd[slot, tm - SUBLANES:tm, :]
            carry_ref[fb, slot] = jnp.where(g == 0, kept, last)
            t_ref[0] = last

        acts = []
        for r0 in range(0, tm, FFN_ROW_CHUNK):
            def conv_half(slot, cw_ref, cb_ref):
                if r0 == 0:
                    ext = jnp.concatenate([prevs[slot], u_rd[slot, 0:FFN_ROW_CHUNK, :]], axis=0)
                else:
                    ext = u_rd[slot, r0 - SUBLANES:r0 + FFN_ROW_CHUNK, :]
                u = ext[SUBLANES:]
                u1 = pltpu.roll(ext, 1, 0)[SUBLANES:]
                u2 = pltpu.roll(ext, 2, 0)[SUBLANES:]
                return cb_ref[...] + cw_ref[0:1, :] * u2 + cw_ref[1:2, :] * u1 + cw_ref[2:3, :] * u

            gate = conv_half(0, cwg_ref, cbg_ref)
            val = conv_half(1, cwv_ref, cbv_ref)
            acts.append(jnp.where(g == 0, 0.0, gate * _sigmoid(gate) * val).astype(BF16))

        acc_ref[...] += jnp.dot(jnp.concatenate(acts, axis=0), wd_ref[...], preferred_element_type=F32)

    @pl.when(lax.rem(g, 2) == 0)
    def _():
        step(ub_ref, ua_ref)

    @pl.when(lax.rem(g, 2) == 1)
    def _():
        step(ua_ref, ub_ref)

    @pl.when(jnp.logical_and(fb == nf - 1, g >= 1))
    def _():
        y_ref[0] = x1_ref[0] + g2_ref[0] * acc_ref[...]
        acc_ref[...] = jnp.zeros(acc_ref.shape, F32)


def _conv_ffn_prompt(h2, x1, mod, w_up, w_down, cw, cb, *, tm, tf):
    bsz, seq, d = x1.shape
    nf = D_FF // tf
    nm = seq // tm
    total = nm * nf
    up = lambda g: jnp.minimum(g, total - 1)
    mid = lambda g: jnp.clip(g - 1, 0, total - 1)
    tile = lambda s: s // nf
    chunk = lambda s: lax.rem(s, nf)
    in_specs = [pl.BlockSpec((1, tm, d), lambda b, g: (b, tile(up(g)), 0)),
                pl.BlockSpec((1, tm, d), lambda b, g: (b, tile(mid(g)), 0)),
                pl.BlockSpec((1, 1, d), lambda b, g: (b, 0, 5)),
                pl.BlockSpec((d, tf), lambda b, g: (0, chunk(up(g)))),
                pl.BlockSpec((d, tf), lambda b, g: (0, nf + chunk(up(g)))),
                pl.BlockSpec((tf, d), lambda b, g: (chunk(mid(g)), 0)),
                pl.BlockSpec((FFN_CONV_WIDTH, tf), lambda b, g: (0, chunk(mid(g)))),
                pl.BlockSpec((FFN_CONV_WIDTH, tf), lambda b, g: (0, nf + chunk(mid(g)))),
                pl.BlockSpec((1, tf), lambda b, g: (0, chunk(mid(g)))),
                pl.BlockSpec((1, tf), lambda b, g: (0, nf + chunk(mid(g))))]
    tail_spec = pl.BlockSpec((1, SUBLANES, tf), lambda b, g: (b, tile(mid(g)), chunk(mid(g))))
    tail_shape = jax.ShapeDtypeStruct((bsz, nm * SUBLANES, D_FF), F32)
    y, tail_g, tail_v = pl.pallas_call(
        functools.partial(_ffn_pipe_kernel, tm=tm, nf=nf, total=total),
        grid=(bsz, total + 1),
        in_specs=in_specs,
        out_specs=[pl.BlockSpec((1, tm, d), lambda b, g: (b, tile(mid(g)), 0)), tail_spec, tail_spec],
        out_shape=[jax.ShapeDtypeStruct((bsz, seq, d), F32), tail_shape, tail_shape],
        scratch_shapes=[pltpu.VMEM((tm, d), F32),
                        pltpu.VMEM((2, tm, tf), F32), pltpu.VMEM((2, tm, tf), F32),
                        pltpu.VMEM((nf, 2, SUBLANES, tf), F32)],
        compiler_params=_params(("arbitrary", "arbitrary")),
        name="conv_ffn_pipe",
    )(h2, x1, mod, w_up, w_up, w_down, cw, cw, cb, cb)
    last = (nm - 1) * SUBLANES
    return y, jnp.concatenate([tail_g[:, last:], tail_v[:, last:]], axis=-1)


def _rope_tables(pos):
    half = HEAD_DIM // 2
    inv = ROPE_THETA ** (-jnp.arange(half, dtype=F32) / half)
    ang = pos.astype(F32)[:, None] * inv[None, :]
    cos = jnp.cos(ang)
    sin = jnp.sin(ang)
    cos_l = jnp.concatenate([cos, cos, cos, cos], axis=-1)
    sin_l = jnp.concatenate([-sin, sin, -sin, sin], axis=-1)
    return cos_l, sin_l


def _head_mean_matrix():
    h = np.arange(D_ATTN) // HEAD_DIM
    return jnp.asarray((h[:, None] == h[None, :]).astype(np.float32) / HEAD_DIM, dtype=BF16)


def _head_expand_matrix():
    h = np.arange(D_ATTN) // HEAD_DIM
    return jnp.asarray((np.arange(N_HEADS)[:, None] == h[None, :]).astype(np.float32), dtype=BF16)


def kernel(x_prompt, x_sample, cache_win_k, cache_win_v, state_conv_a, state_ffn_conv, c_prompt, c_sample,
           norm_mix_g, norm_ffn_g, w_ada, b_ada, w_in, conv_a_w, conv_a_b, ln_a_g, ln_a_b, q_norm_g, k_norm_g,
           w_out, w_up, ffn_conv_w, ffn_conv_b, w_down):
    bsz, seq, d = x_prompt.shape
    bd, dt, _ = x_sample.shape
    assert w_ada.shape[0] == 1 and dt == DEC_T and cache_win_k.shape[2] == WIN_MAX and seq == 2 * WIN_MAX

    w_in_b = w_in[0]
    w_out_b = w_out[0]
    w_up_b = w_up[0].astype(BF16)
    w_down_b = w_down[0].astype(BF16)
    e_mean = _head_mean_matrix()
    e16 = _head_expand_matrix()
    qg = jnp.tile(q_norm_g[0], N_HEADS)[None, :]
    kg = jnp.tile(k_norm_g[0], N_HEADS)[None, :]

    c_all = jnp.concatenate([c_prompt, c_sample, jnp.zeros((16 - bsz - bd, d), F32)], axis=0)
    mod = _ada(c_all, w_ada[0], b_ada)
    mod_p = mod[:bsz][:, None, :]
    mod_s = jnp.tile(mod[bsz:bsz + bd], (dt, 1))[None]

    cos_p, sin_p = _rope_tables(jnp.arange(seq))
    g_p, *qkv_views, g_tail, k_tail, v_tail = _in_proj(
        x_prompt, mod_p, mod_p, norm_mix_g, w_in_b, qg, kg, cos_p, sin_p, e_mean,
        tm=256, out_dtype=BF16, tails=True)
    a_p = _conv_mix(g_p, None, conv_a_w[0], conv_a_b, ln_a_g, ln_a_b, tc=256, rs=1, out_dtype=BF16)
    outs, lses = [], []
    for n, ((window, dil), width, ub) in enumerate(zip(DIL_PATTERNS, (256, 512, 512), (2, 1, 1))):
        o, lse = _band_attn(*qkv_views[3 * n:3 * n + 3], dil, width, ub)
        outs.append(o)
        lses.append(lse)
    x1_p, h2_p = _out_proj(a_p, outs, lses, x_prompt, mod_p, norm_ffn_g, w_out_b, e16, tm=256)
    y_p, ftail_p = _conv_ffn_prompt(h2_p, x1_p, mod_p, w_up_b, w_down_b, ffn_conv_w[0], ffn_conv_b,
                                    tm=512, tf=512)

    rows = dt * bd
    to_tm = lambda a: a.transpose(1, 0, 2).reshape(1, a.shape[0] * a.shape[1], a.shape[2])
    xs = to_tm(x_sample)
    cos_s, sin_s = _rope_tables(PAST_LEN + jnp.arange(rows) // bd)
    g_s, q_s, k_s, v_s = _in_proj(xs, mod_s, mod_s, norm_mix_g, w_in_b, qg, kg, cos_s, sin_s, e_mean,
                                  tm=rows, out_dtype=F32, tails=False)
    a_s = _conv_mix(g_s, to_tm(state_conv_a[0]), conv_a_w[0], conv_a_b, ln_a_g, ln_a_b,
                    tc=rows, rs=bd, out_dtype=BF16)
    heads = lambda a: a.reshape(dt, bd, N_HEADS, HEAD_DIM)
    ng = N_HEADS // HEAD_GROUP
    qg5 = heads(q_s).reshape(dt, bd, ng, HEAD_GROUP, HEAD_DIM).transpose(1, 2, 3, 0, 4)
    q_blk = (qg5[:, :, :, :, None, :] * jnp.eye(HEAD_GROUP, dtype=F32)[None, None, :, None, :, None])
    q_blk = q_blk.reshape(bd, ng, HEAD_GROUP * dt, GROUP_COLS).astype(BF16)
    to_pos_minor = lambda c: c[0].transpose(0, 2, 3, 1).reshape(bd, D_ATTN, WIN_MAX)
    new_tile = lambda a: jnp.pad(heads(a).transpose(1, 2, 3, 0).reshape(bd, D_ATTN, dt),
                                 ((0, 0), (0, 0), (NEW_LANE0, 0)))
    o_s, win_kt, win_vt = _sample_attn(q_blk, to_pos_minor(cache_win_k), to_pos_minor(cache_win_v),
                                       new_tile(k_s), new_tile(v_s))
    o_s = o_s.transpose(1, 0, 2).reshape(1, rows, D_ATTN)
    from_pos_minor = lambda a: a.reshape(a.shape[0], N_HEADS, HEAD_DIM, a.shape[-1]).transpose(0, 3, 1, 2)[None]
    win_k_s, win_v_s = from_pos_minor(win_kt), from_pos_minor(win_vt)
    x1_s, h2_s = _out_proj(a_s, [o_s], None, xs, mod_s, norm_ffn_g, w_out_b, e16, tm=rows)
    y_s, ftail_s = _conv_ffn(h2_s, x1_s, mod_s, w_up_b, w_down_b, ffn_conv_w[0], ffn_conv_b,
                             to_tm(state_ffn_conv[0]), tm=rows, tf=512, rs=bd)

    from_tm = lambda a, t: a.reshape(t, bd, a.shape[-1]).transpose(1, 0, 2)
    conv_a_s = jnp.concatenate([state_conv_a[0], from_tm(g_s[0], dt)], axis=1)[:, dt:]
    return (y_p, from_tm(y_s[0], dt),
            from_pos_minor(k_tail), from_pos_minor(v_tail),
            g_tail[:, 32 - (CONV_A_WIDTH - 1):][None],
            ftail_p[:, SUBLANES - (FFN_CONV_WIDTH - 1):][None],
            win_k_s, win_v_s,
            conv_a_s[None],
            from_tm(ftail_s[0], FFN_CONV_WIDTH - 1)[None])
```
